```python
import math
import jax, jax.numpy as jnp
from jax import lax
import numpy as np

D_MODEL = 1024
BATCH = 16
SEQ = 4096
DEPTH = 4

BRANCH_W = D_MODEL // 2
N_BRANCH = 3
CONV_W = BRANCH_W
CONV_K = 31
DA_HEADS = 4
DA_HEAD_DIM = BRANCH_W // DA_HEADS // 2
DA_V_DIM = 2 * DA_HEAD_DIM
Q_BLOCK = 128
SGU_W = BRANCH_W
SGU_GROUPS = 4
SGU_GROUP_DIM = SGU_W // SGU_GROUPS
CHUNK = 128
IN_COLS = 3 * CONV_W + 4 * BRANCH_W + 3 * SGU_W + N_BRANCH * D_MODEL
EPS = 1e-6

kernel_name = "hybrid_conv_diffattn_sgu_gated_trunk"


def rms_norm(x, g):
    xf = x.astype(jnp.float32)
    y = xf * lax.rsqrt(jnp.mean(xf * xf, axis=-1, keepdims=True) + EPS)
    return (y * g.astype(jnp.float32)).astype(x.dtype)


def layer_norm(x, g, b):
    xf = x.astype(jnp.float32)
    mu = jnp.mean(xf, axis=-1, keepdims=True)
    var = jnp.mean(jnp.square(xf - mu), axis=-1, keepdims=True)
    y = (xf - mu) * lax.rsqrt(var + EPS)
    return (y * g.astype(jnp.float32) + b.astype(jnp.float32)).astype(x.dtype)


def split_columns(proj):
    sizes = [2 * CONV_W, CONV_W,
             BRANCH_W, BRANCH_W, BRANCH_W, BRANCH_W,
             SGU_W, SGU_W, SGU_W,
             N_BRANCH * D_MODEL]
    idx = list(np.cumsum(sizes)[:-1])
    return jnp.split(proj, idx, axis=-1)


def conv_branch(a_glu, gate, conv_w, conv_b, ln_g, ln_b):
    a, g = jnp.split(a_glu, 2, axis=-1)
    z = a * jax.nn.sigmoid(g)
    z = lax.conv_general_dilated(
        z, conv_w[:, None, :].astype(z.dtype), window_strides=(1,),
        padding=[(CONV_K - 1, 0)],
        dimension_numbers=('NWC', 'WIO', 'NWC'),
        feature_group_count=CONV_W) + conv_b
    z = jax.nn.silu(layer_norm(z, ln_g, ln_b))
    return z * jax.nn.silu(gate)


def diff_attention(q, k, v, gate, lam_q1, lam_k1, lam_q2, lam_k2, sub_g, lambda_init):
    B, S, _ = q.shape
    q = q.reshape(B, S, DA_HEADS, 2, DA_HEAD_DIM)
    k = k.reshape(B, S, DA_HEADS, 2, DA_HEAD_DIM)
    v = v.reshape(B, S, DA_HEADS, DA_V_DIM)
    scale = DA_HEAD_DIM ** -0.5
    lam = (jnp.exp(jnp.sum(lam_q1.astype(jnp.float32) * lam_k1.astype(jnp.float32)))
           - jnp.exp(jnp.sum(lam_q2.astype(jnp.float32) * lam_k2.astype(jnp.float32)))
           + lambda_init)
    n_blocks = S // Q_BLOCK
    qb = q.reshape(B, n_blocks, Q_BLOCK, DA_HEADS, 2, DA_HEAD_DIM).transpose(1, 0, 2, 3, 4, 5)
    kpos = jnp.arange(S)

    def one_block(args):
        qi, bidx = args
        s = jnp.einsum('bqhpd,bkhpd->bhpqk', qi, k).astype(jnp.float32) * scale
        qpos = bidx * Q_BLOCK + jnp.arange(Q_BLOCK)
        mask = kpos[None, :] <= qpos[:, None]
        s = jnp.where(mask, s, -jnp.inf)
        p = jax.nn.softmax(s, axis=-1)
        attn = p[:, :, 0] - lam * p[:, :, 1]
        return jnp.einsum('bhqk,bkhe->bqhe', attn.astype(v.dtype), v)

    o = lax.map(one_block, (qb, jnp.arange(n_blocks)))
    o = o.transpose(1, 0, 2, 3, 4).reshape(B, S, DA_HEADS, DA_V_DIM)
    o = rms_norm(o, sub_g) * (1.0 - lambda_init)
    return o.reshape(B, S, DA_HEADS * DA_V_DIM) * jax.nn.silu(gate)


def sgu_branch(u, v, gate, ln_g, ln_b, w_s, b_s):
    B, S, _ = v.shape
    v = layer_norm(v, ln_g, ln_b)
    vc = v.reshape(B, S // CHUNK, CHUNK, SGU_GROUPS, SGU_GROUP_DIM)
    causal = jnp.tril(jnp.ones((CHUNK, CHUNK), dtype=w_s.dtype))
    w = w_s * causal[None]
    mixed = jnp.einsum('gts,bnsgc->bntgc', w, vc) + b_s.T[None, None, :, :, None]
    return u * mixed.reshape(B, S, SGU_W) * jax.nn.silu(gate)


def setup_inputs(seed: int = 0) -> dict:
    key = jax.random.key(seed)
    ks = jax.random.split(key, 21)
    f32 = jnp.float32
    nrm = lambda k, shape, scale: jax.random.normal(k, shape, f32) * scale
    L = DEPTH
    return {
        "x": jax.random.normal(ks[0], (BATCH, SEQ, D_MODEL), f32),
        "norm_g": 1.0 + nrm(ks[1], (L, D_MODEL), 0.02),
        "w_in": nrm(ks[2], (L, D_MODEL, IN_COLS), D_MODEL ** -0.5),
        "conv_w": nrm(ks[3], (L, CONV_K, CONV_W), CONV_K ** -0.5),
        "conv_b": nrm(ks[4], (L, CONV_W), 0.02),
        "conv_ln_g": 1.0 + nrm(ks[5], (L, CONV_W), 0.02),
        "conv_ln_b": nrm(ks[6], (L, CONV_W), 0.02),
        "lam_q1": nrm(ks[7], (L, DA_HEAD_DIM), 0.1),
        "lam_k1": nrm(ks[8], (L, DA_HEAD_DIM), 0.1),
        "lam_q2": nrm(ks[9], (L, DA_HEAD_DIM), 0.1),
        "lam_k2": nrm(ks[10], (L, DA_HEAD_DIM), 0.1),
        "diff_norm_g": 1.0 + nrm(ks[11], (L, DA_V_DIM), 0.02),
        "sgu_ln_g": 1.0 + nrm(ks[12], (L, SGU_W), 0.02),
        "sgu_ln_b": nrm(ks[13], (L, SGU_W), 0.02),
        "sgu_w": nrm(ks[14], (L, SGU_GROUPS, CHUNK, CHUNK), CHUNK ** -0.5),
        "sgu_b": 1.0 + nrm(ks[15], (L, SGU_GROUPS, CHUNK), 0.02),
        "w_pa": nrm(ks[16], (L, CONV_W, D_MODEL), CONV_W ** -0.5),
        "w_pb": nrm(ks[17], (L, BRANCH_W, D_MODEL), BRANCH_W ** -0.5),
        "w_pc": nrm(ks[18], (L, SGU_W, D_MODEL), SGU_W ** -0.5),
        "w_o": nrm(ks[19], (L, D_MODEL, D_MODEL), D_MODEL ** -0.5),
        "final_norm_g": 1.0 + nrm(ks[20], (D_MODEL,), 0.02),
    }


def reference(x, norm_g, w_in, conv_w, conv_b, conv_ln_g, conv_ln_b, lam_q1, lam_k1,
              lam_q2, lam_k2, diff_norm_g, sgu_ln_g, sgu_ln_b, sgu_w, sgu_b,
              w_pa, w_pb, w_pc, w_o, final_norm_g):
    for l in range(DEPTH):
        lambda_init = 0.8 - 0.6 * math.exp(-0.3 * l)
        h = rms_norm(x, norm_g[l])
        proj = jnp.einsum('bsd,dn->bsn', h, w_in[l])
        (a_glu, a_gate, q, k, v, b_gate, u, sv, c_gate, gates) = split_columns(proj)
        ya = conv_branch(a_glu, a_gate, conv_w[l], conv_b[l], conv_ln_g[l], conv_ln_b[l]) @ w_pa[l]
        yb = diff_attention(q, k, v, b_gate, lam_q1[l], lam_k1[l], lam_q2[l], lam_k2[l],
                            diff_norm_g[l], lambda_init) @ w_pb[l]
        yc = sgu_branch(u, sv, c_gate, sgu_ln_g[l], sgu_ln_b[l], sgu_w[l], sgu_b[l]) @ w_pc[l]
        ga, gb, gc = jnp.split(jax.nn.sigmoid(gates), N_BRANCH, axis=-1)
        merged = ga * ya + gb * yb + gc * yc
        x = x + merged @ w_o[l]
    return rms_norm(x, final_norm_g)
```

```python
import functools
import math

import jax
import jax.numpy as jnp
from jax import lax
from jax.experimental import pallas as pl
from jax.experimental.pallas import tpu as pltpu

D_MODEL = 1024
BRANCH_W = D_MODEL // 2
CONV_K = 31
DA_HEADS = 4
DA_HEAD_DIM = 64
DA_V_DIM = 128
SGU_GROUPS = 4
CHUNK = 128
IN_COLS = 8192
EPS = 1e-6

LANE = 128
COL_AGLU = 0
COL_AGATE = 1024 // LANE
COL_Q = 1536 // LANE
COL_K = 2048 // LANE
COL_V = 2560 // LANE
COL_BGATE = 3072 // LANE
COL_U = 3584 // LANE
COL_SV = 4096 // LANE
COL_CGATE = 4608 // LANE
COL_GATES = 5120 // LANE

HALO = 32
NEG = -1e30

F32 = jnp.float32
BF16 = jnp.bfloat16

VMEM_LIMIT = 56 * 1024 * 1024


def _inproj_kernel(x_ref, g_ref, w_ref, o_ref, h_scr):
    @pl.when(pl.program_id(1) == 0)
    def _():
        x = x_ref[...]
        ms = jnp.mean(x * x, axis=-1, keepdims=True)
        h_scr[...] = (x * lax.rsqrt(ms + EPS) * g_ref[...]).astype(BF16)

    o_ref[...] = jnp.dot(h_scr[...], w_ref[...], preferred_element_type=F32).astype(BF16)


def _inproj(x2d, g, w_bf16, layer, tm=1024, tn=1024):
    m = x2d.shape[0]
    return pl.pallas_call(
        _inproj_kernel,
        grid=(m // tm, IN_COLS // tn),
        in_specs=[
            pl.BlockSpec((tm, D_MODEL), lambda i, j: (i, 0)),
            pl.BlockSpec((None, 1, D_MODEL), lambda i, j: (layer, 0, 0)),
            pl.BlockSpec((None, D_MODEL, tn), lambda i, j: (layer, 0, j)),
        ],
        out_specs=pl.BlockSpec((tm, tn), lambda i, j: (i, j)),
        out_shape=jax.ShapeDtypeStruct((m, IN_COLS), BF16),
        scratch_shapes=[pltpu.VMEM((tm, D_MODEL), BF16)],
        compiler_params=pltpu.CompilerParams(
            dimension_semantics=("arbitrary", "arbitrary"), vmem_limit_bytes=VMEM_LIMIT),
        name="inproj",
    )(x2d, g, w_bf16)


def _attn_kernel(lq1_ref, lk1_ref, lq2_ref, lk2_ref, sg_ref, q_ref, k_ref, v_ref, o_ref,
                 *, lambda_init, tq):
    i = pl.program_id(2)
    lam = (jnp.exp(jnp.sum(lq1_ref[...] * lk1_ref[...], axis=-1, keepdims=True))
           - jnp.exp(jnp.sum(lq2_ref[...] * lk2_ref[...], axis=-1, keepdims=True))
           + lambda_init)

    q = q_ref[...] * jnp.asarray(DA_HEAD_DIM ** -0.5, BF16)
    lane = lax.broadcasted_iota(jnp.int32, q.shape, 1)
    zero = jnp.zeros_like(q)
    qz = (jnp.where(lane < DA_HEAD_DIM, q, zero), jnp.where(lane >= DA_HEAD_DIM, q, zero))

    row = lax.broadcasted_iota(jnp.int32, (tq, tq), 0)
    col = lax.broadcasted_iota(jnp.int32, (tq, tq), 1)
    causal = col <= row

    def block(j, carry, masked):
        start = pl.multiple_of(j * tq, tq)
        kb = k_ref[pl.ds(start, tq), :]
        vb = v_ref[pl.ds(start, tq), :]
        out = []
        for p_idx in range(2):
            m, l, a = carry[p_idx]
            s = lax.dot_general(qz[p_idx], kb, (((1,), (1,)), ((), ())),
                                preferred_element_type=F32)
            if masked:
                s = jnp.where(causal, s, NEG)
            m_new = jnp.maximum(m, jnp.max(s, axis=-1, keepdims=True))
            alpha = jnp.exp(m - m_new)
            p = jnp.exp(s - m_new)
            l = alpha * l + jnp.sum(p, axis=-1, keepdims=True)
            a = alpha * a + jnp.dot(p.astype(BF16), vb, preferred_element_type=F32)
            out.append((m_new, l, a))
        return tuple(out)

    init_one = (jnp.full((tq, 1), NEG, F32), jnp.zeros((tq, 1), F32),
                jnp.zeros((tq, DA_V_DIM), F32))
    carry = lax.fori_loop(0, i, lambda j, c: block(j, c, False), (init_one, init_one))
    (_, l1, a1), (_, l2, a2) = block(i, carry, True)

    o = a1 / l1 - lam * (a2 / l2)
    ms = jnp.mean(o * o, axis=-1, keepdims=True)
    o = o * lax.rsqrt(ms + EPS) * sg_ref[...] * (1.0 - lambda_init)
    o_ref[...] = o.astype(o_ref.dtype)


def _attention(proj3d, lq1, lk1, lq2, lk2, sub_g, layer, lambda_init, tq=256):
    b, s, _ = proj3d.shape
    vec = lambda n: pl.BlockSpec((None, 1, n), lambda bi, h, i: (layer, 0, 0))
    return pl.pallas_call(
        functools.partial(_attn_kernel, lambda_init=lambda_init, tq=tq),
        grid=(b, DA_HEADS, s // tq),
        in_specs=[
            vec(DA_HEAD_DIM), vec(DA_HEAD_DIM), vec(DA_HEAD_DIM), vec(DA_HEAD_DIM), vec(DA_V_DIM),
            pl.BlockSpec((None, tq, LANE), lambda bi, h, i: (bi, i, COL_Q + h)),
            pl.BlockSpec((None, s, LANE), lambda bi, h, i: (bi, 0, COL_K + h)),
            pl.BlockSpec((None, s, LANE), lambda bi, h, i: (bi, 0, COL_V + h)),
        ],
        out_specs=pl.BlockSpec((None, tq, LANE), lambda bi, h, i: (bi, i, h)),
        out_shape=jax.ShapeDtypeStruct((b, s, BRANCH_W), BF16),
        compiler_params=pltpu.CompilerParams(
            dimension_semantics=("arbitrary", "arbitrary", "arbitrary"),
            vmem_limit_bytes=VMEM_LIMIT),
        name="diff_attn",
    )(lq1, lk1, lq2, lk2, sub_g, proj3d, proj3d, proj3d)


def _layer_norm(v, g, b):
    mu = jnp.mean(v, axis=-1, keepdims=True)
    d = v - mu
    var = jnp.mean(d * d, axis=-1, keepdims=True)
    return d * lax.rsqrt(var + EPS) * g + b


def _silu(v):
    return v * jax.nn.sigmoid(v)


def _merge_kernel(x_ref, aglu_ref, agate_ref, bgate_ref, u_ref, sv_ref, cgate_ref,
                  ga_ref, gb_ref, gc_ref, o_ref,
                  convw_ref, convb_ref, clng_ref, clnb_ref, slng_ref, slnb_ref, sguw_ref, sgub_ref,
                  wpa_ref, wpb_ref, wpc_ref, wo_ref, fng_ref, out_ref, zbuf, conv_scr,
                  *, t_rows, final):
    @pl.when(pl.program_id(1) == 0)
    def _():
        zbuf[0:HALO, :] = jnp.zeros((HALO, BRANCH_W), F32)

    a = aglu_ref[:, :BRANCH_W].astype(F32)
    g = aglu_ref[:, BRANCH_W:].astype(F32)
    zbuf[HALO:HALO + t_rows, :] = a * jax.nn.sigmoid(g)
    rc = 64
    for c in range(BRANCH_W // LANE):
        cs = slice(c * LANE, (c + 1) * LANE)
        for r in range(t_rows // rc):
            acc = jnp.broadcast_to(convb_ref[:, cs], (rc, LANE))
            for j in range(CONV_K):
                off = r * rc + HALO - (CONV_K - 1) + j
                acc = acc + convw_ref[j:j + 1, cs] * zbuf[off:off + rc, cs]
            conv_scr[r * rc:(r + 1) * rc, cs] = acc
    zbuf[0:HALO, :] = zbuf[t_rows:t_rows + HALO, :]
    za = _silu(_layer_norm(conv_scr[...], clng_ref[...], clnb_ref[...]))
    za = za * _silu(agate_ref[...].astype(F32))
    ya = jnp.dot(za.astype(BF16), wpa_ref[...], preferred_element_type=F32)

    zb = o_ref[...].astype(F32) * _silu(bgate_ref[...].astype(F32))
    yb = jnp.dot(zb.astype(BF16), wpb_ref[...], preferred_element_type=F32)

    v = _layer_norm(sv_ref[...].astype(F32), slng_ref[...], slnb_ref[...]).astype(BF16)
    row = lax.broadcasted_iota(jnp.int32, (CHUNK, CHUNK), 0)
    col = lax.broadcasted_iota(jnp.int32, (CHUNK, CHUNK), 1)
    tril = col <= row
    ws = [jnp.where(tril, sguw_ref[gi], 0.0).astype(BF16) for gi in range(SGU_GROUPS)]
    bs = [jnp.broadcast_to(sgub_ref[:, gi:gi + 1], (CHUNK, LANE)) for gi in range(SGU_GROUPS)]
    rows = []
    for n in range(t_rows // CHUNK):
        blocks = []
        for gi in range(SGU_GROUPS):
            vb = v[n * CHUNK:(n + 1) * CHUNK, gi * LANE:(gi + 1) * LANE]
            blocks.append(jnp.dot(ws[gi], vb, preferred_element_type=F32) + bs[gi])
        rows.append(jnp.concatenate(blocks, axis=1))
    mixed = jnp.concatenate(rows, axis=0)
    zc = u_ref[...].astype(F32) * mixed * _silu(cgate_ref[...].astype(F32))
    yc = jnp.dot(zc.astype(BF16), wpc_ref[...], preferred_element_type=F32)

    ga = jax.nn.sigmoid(ga_ref[...].astype(F32))
    gb = jax.nn.sigmoid(gb_ref[...].astype(F32))
    gc = jax.nn.sigmoid(gc_ref[...].astype(F32))
    merged = ga * ya + gb * yb + gc * yc
    xn = x_ref[...] + jnp.dot(merged.astype(BF16), wo_ref[...], preferred_element_type=F32)
    if final:
        ms = jnp.mean(xn * xn, axis=-1, keepdims=True)
        xn = xn * lax.rsqrt(ms + EPS) * fng_ref[...]
    out_ref[...] = xn


def _merge(x3d, proj3d, o3d, conv_w, conv_b, cln_g, cln_b, sln_g, sln_b, sgu_w, sgu_bt,
           w_pa, w_pb, w_pc, w_o, fn_g, layer, final, t_rows=512):
    b, s, _ = x3d.shape

    def pcols(col, width):
        return pl.BlockSpec((None, t_rows, width), lambda bi, t: (bi, t, (col * LANE) // width))

    def lvec(n):
        return pl.BlockSpec((None, 1, n), lambda bi, t: (layer, 0, 0))

    def lmat(r, c):
        return pl.BlockSpec((None, r, c), lambda bi, t: (layer, 0, 0))

    return pl.pallas_call(
        functools.partial(_merge_kernel, t_rows=t_rows, final=final),
        grid=(b, s // t_rows),
        in_specs=[
            pl.BlockSpec((None, t_rows, D_MODEL), lambda bi, t: (bi, t, 0)),
            pcols(COL_AGLU, 2 * BRANCH_W),
            pcols(COL_AGATE, BRANCH_W),
            pcols(COL_BGATE, BRANCH_W),
            pcols(COL_U, BRANCH_W),
            pcols(COL_SV, BRANCH_W),
            pcols(COL_CGATE, BRANCH_W),
            pcols(COL_GATES, D_MODEL),
            pcols(COL_GATES + D_MODEL // LANE, D_MODEL),
            pcols(COL_GATES + 2 * D_MODEL // LANE, D_MODEL),
            pl.BlockSpec((None, t_rows, BRANCH_W), lambda bi, t: (bi, t, 0)),
            lmat(CONV_K, BRANCH_W),
            lvec(BRANCH_W), lvec(BRANCH_W), lvec(BRANCH_W), lvec(BRANCH_W), lvec(BRANCH_W),
            pl.BlockSpec((None, SGU_GROUPS, CHUNK, CHUNK), lambda bi, t: (layer, 0, 0, 0)),
            lmat(CHUNK, SGU_GROUPS),
            lmat(BRANCH_W, D_MODEL), lmat(BRANCH_W, D_MODEL), lmat(BRANCH_W, D_MODEL),
            lmat(D_MODEL, D_MODEL),
            pl.BlockSpec((1, D_MODEL), lambda bi, t: (0, 0)),
        ],
        out_specs=pl.BlockSpec((None, t_rows, D_MODEL), lambda bi, t: (bi, t, 0)),
        out_shape=jax.ShapeDtypeStruct((b, s, D_MODEL), F32),
        scratch_shapes=[pltpu.VMEM((t_rows + HALO, BRANCH_W), F32),
                        pltpu.VMEM((t_rows, BRANCH_W), F32)],
        compiler_params=pltpu.CompilerParams(
            dimension_semantics=("arbitrary", "arbitrary"), vmem_limit_bytes=VMEM_LIMIT),
        name="merge",
    )(x3d, proj3d, proj3d, proj3d, proj3d, proj3d, proj3d, proj3d, proj3d, proj3d, o3d,
      conv_w, conv_b, cln_g, cln_b, sln_g, sln_b, sgu_w, sgu_bt, w_pa, w_pb, w_pc, w_o, fn_g)


def kernel(x, norm_g, w_in, conv_w, conv_b, conv_ln_g, conv_ln_b, lam_q1, lam_k1, lam_q2, lam_k2,
           diff_norm_g, sgu_ln_g, sgu_ln_b, sgu_w, sgu_b, w_pa, w_pb, w_pc, w_o, final_norm_g):
    b, s, d = x.shape
    depth = w_in.shape[0]
    row3 = lambda p: p[:, None, :]
    w_in_b, w_pa_b, w_pb_b, w_pc_b, w_o_b = (w.astype(BF16) for w in (w_in, w_pa, w_pb, w_pc, w_o))
    norm_g3, conv_b3, cln_g3, cln_b3 = row3(norm_g), row3(conv_b), row3(conv_ln_g), row3(conv_ln_b)
    sln_g3, sln_b3, sub_g3 = row3(sgu_ln_g), row3(sgu_ln_b), row3(diff_norm_g)
    lq1, lk1, lq2, lk2 = row3(lam_q1), row3(lam_k1), row3(lam_q2), row3(lam_k2)
    sgu_bt = jnp.swapaxes(sgu_b, 1, 2)
    fn_g = final_norm_g[None, :]

    for layer in range(depth):
        lambda_init = 0.8 - 0.6 * math.exp(-0.3 * layer)
        proj = _inproj(x.reshape(b * s, d), norm_g3, w_in_b, layer).reshape(b, s, IN_COLS)
        o = _attention(proj, lq1, lk1, lq2, lk2, sub_g3, layer, lambda_init)
        x = _merge(x, proj, o, conv_w, conv_b3, cln_g3, cln_b3, sln_g3, sln_b3, sgu_w, sgu_bt,
                   w_pa_b, w_pb_b, w_pc_b, w_o_b, fn_g, layer, final=(layer == depth - 1))
    return x
```

```python
import functools
import math

import jax
import jax.numpy as jnp
from jax import lax
from jax.experimental import pallas as pl
from jax.experimental.pallas import tpu as pltpu

D_MODEL = 1024
BRANCH_W = D_MODEL // 2
CONV_K = 31
DA_HEADS = 4
DA_HEAD_DIM = 64
DA_V_DIM = 128
SGU_GROUPS = 4
CHUNK = 128
IN_COLS = 8192
EPS = 1e-6

LANE = 128
SUBLANE = 8
COL_AGLU = 0
COL_AGATE = 1024 // LANE
COL_Q = 1536 // LANE
COL_K = 2048 // LANE
COL_V = 2560 // LANE
COL_BGATE = 3072 // LANE
COL_U = 3584 // LANE
COL_SV = 4096 // LANE
COL_CGATE = 4608 // LANE
COL_GATES = 5120 // LANE

ONES_ROWS = 16
Q_SCALE = DA_HEAD_DIM ** -0.5 * math.log2(math.e)
HALO = 32
NEG = -1e30

F32 = jnp.float32
BF16 = jnp.bfloat16

VMEM_LIMIT = 56 * 1024 * 1024


def _inproj_kernel(x_ref, g_ref, w_ref, cs_ref, o_ref, h_scr):
    @pl.when(pl.program_id(1) == 0)
    def _():
        x = x_ref[...]
        ms = jnp.mean(x * x, axis=-1, keepdims=True)
        h_scr[...] = (x * lax.rsqrt(ms + EPS) * g_ref[...]).astype(BF16)

    acc = jnp.dot(h_scr[...], w_ref[...], preferred_element_type=F32)
    o_ref[...] = (acc * cs_ref[...]).astype(BF16)


def _inproj(x2d, g, w_bf16, col_scale, layer, tm=1024, tn=1024):
    m = x2d.shape[0]
    return pl.pallas_call(
        _inproj_kernel,
        grid=(m // tm, IN_COLS // tn),
        in_specs=[
            pl.BlockSpec((tm, D_MODEL), lambda i, j: (i, 0)),
            pl.BlockSpec((None, 1, D_MODEL), lambda i, j: (layer, 0, 0)),
            pl.BlockSpec((None, D_MODEL, tn), lambda i, j: (layer, 0, j)),
            pl.BlockSpec((1, tn), lambda i, j: (0, j)),
        ],
        out_specs=pl.BlockSpec((tm, tn), lambda i, j: (i, j)),
        out_shape=jax.ShapeDtypeStruct((m, IN_COLS), BF16),
        scratch_shapes=[pltpu.VMEM((tm, D_MODEL), BF16)],
        compiler_params=pltpu.CompilerParams(
            dimension_semantics=("arbitrary", "arbitrary"), vmem_limit_bytes=VMEM_LIMIT),
        name="inproj",
    )(x2d, g, w_bf16, col_scale)


def _attn_kernel(lq1_ref, lk1_ref, lq2_ref, lk2_ref, sg_ref, q_ref, k_ref, v_ref, o_ref,
                 vt_scr, acc_scr, *, lambda_init, tb):
    i = pl.program_id(1)
    n_kv = v_ref.shape[0] // tb
    n_chain = 2 * DA_HEADS

    @pl.when(i == 0)
    def _():
        ones = jnp.ones((ONES_ROWS, tb), BF16)

        def fill(c, carry):
            vblk = v_ref[pl.ds(pl.multiple_of(c * tb, tb), tb), :]
            vt = vblk.astype(F32).T.astype(BF16)
            for h in range(DA_HEADS):
                vt_scr[c, h, 0:DA_V_DIM, :] = vt[h * DA_V_DIM:(h + 1) * DA_V_DIM, :]
                vt_scr[c, h, DA_V_DIM:, :] = ones
            return carry
        lax.fori_loop(0, n_kv, fill, 0)

    lam = (jnp.exp(jnp.sum(lq1_ref[...] * lk1_ref[...], axis=-1, keepdims=True))
           - jnp.exp(jnp.sum(lq2_ref[...] * lk2_ref[...], axis=-1, keepdims=True))
           + lambda_init)

    qt = q_ref[...].astype(F32).T
    sub = lax.broadcasted_iota(jnp.int32, (LANE, tb), 0)
    qtz = []
    for h in range(DA_HEADS):
        qth = qt[h * LANE:(h + 1) * LANE, :]
        qtz.append(jnp.where(sub < DA_HEAD_DIM, qth, 0.0).astype(BF16))
        qtz.append(jnp.where(sub >= DA_HEAD_DIM, qth, 0.0).astype(BF16))

    key_idx = lax.broadcasted_iota(jnp.int32, (tb, tb), 0)
    qry_idx = lax.broadcasted_iota(jnp.int32, (tb, tb), 1)
    causal = key_idx <= qry_idx

    acc_scr[...] = jnp.zeros(acc_scr.shape, F32)

    def block(j, carry, masked):
        kb = k_ref[pl.ds(pl.multiple_of(j * tb, tb), tb), :]
        sts = [jnp.dot(kb[:, (c // 2) * LANE:(c // 2 + 1) * LANE], qtz[c],
                       preferred_element_type=F32) for c in range(n_chain)]
        pts, alphas, out = [], [], []
        for c in range(n_chain):
            m = carry[c]
            st = jnp.where(causal, sts[c], NEG) if masked else sts[c]
            m_new = jnp.maximum(m, jnp.max(st, axis=0, keepdims=True))
            alphas.append(jnp.exp2(m - m_new))
            pts.append(jnp.exp2(st - m_new).astype(BF16))
            out.append(m_new)
        for c in range(n_chain):
            acc_scr[c] = alphas[c] * acc_scr[c] + jnp.dot(
                vt_scr[j, c // 2], pts[c], preferred_element_type=F32)
        return tuple(out)

    m_init = jnp.full((1, tb), NEG, F32)
    carry = lax.fori_loop(0, i, lambda j, c: block(j, c, False), (m_init,) * n_chain)
    block(i, carry, True)

    def normalised(c):
        return acc_scr[c, 0:DA_V_DIM, :] / acc_scr[c, DA_V_DIM:DA_V_DIM + 1, :]

    for h in range(DA_HEADS):
        ot = normalised(2 * h) - lam * normalised(2 * h + 1)
        o = ot.T
        ms = jnp.mean(o * o, axis=-1, keepdims=True)
        o = o * lax.rsqrt(ms + EPS) * sg_ref[...] * (1.0 - lambda_init)
        o_ref[:, h * DA_V_DIM:(h + 1) * DA_V_DIM] = o.astype(o_ref.dtype)


def _attention(proj3d, lq1, lk1, lq2, lk2, sub_g, layer, lambda_init, tb=256):
    b, s, _ = proj3d.shape
    vec = lambda n: pl.BlockSpec((None, 1, n), lambda bi, i: (layer, 0, 0))
    lanes_per_branch = BRANCH_W // LANE
    return pl.pallas_call(
        functools.partial(_attn_kernel, lambda_init=lambda_init, tb=tb),
        grid=(b, s // tb),
        in_specs=[
            vec(DA_HEAD_DIM), vec(DA_HEAD_DIM), vec(DA_HEAD_DIM), vec(DA_HEAD_DIM), vec(DA_V_DIM),
            pl.BlockSpec((None, tb, BRANCH_W), lambda bi, i: (bi, i, COL_Q // lanes_per_branch)),
            pl.BlockSpec((None, s, BRANCH_W), lambda bi, i: (bi, 0, COL_K // lanes_per_branch)),
            pl.BlockSpec((None, s, BRANCH_W), lambda bi, i: (bi, 0, COL_V // lanes_per_branch)),
        ],
        out_specs=pl.BlockSpec((None, tb, BRANCH_W), lambda bi, i: (bi, i, 0)),
        out_shape=jax.ShapeDtypeStruct((b, s, BRANCH_W), BF16),
        scratch_shapes=[pltpu.VMEM((s // tb, DA_HEADS, DA_V_DIM + ONES_ROWS, tb), BF16),
                        pltpu.VMEM((2 * DA_HEADS, DA_V_DIM + ONES_ROWS, tb), F32)],
        compiler_params=pltpu.CompilerParams(
            dimension_semantics=("arbitrary", "arbitrary"),
            vmem_limit_bytes=VMEM_LIMIT),
        name="diff_attn",
    )(lq1, lk1, lq2, lk2, sub_g, proj3d, proj3d, proj3d)


def _layer_norm(v, g, b):
    mu = jnp.mean(v, axis=-1, keepdims=True)
    d = v - mu
    var = jnp.mean(d * d, axis=-1, keepdims=True)
    return d * lax.rsqrt(var + EPS) * g + b


def _silu(v):
    return v * jax.nn.sigmoid(v)


def _merge_kernel(x_ref, aglu_ref, agate_ref, bgate_ref, u_ref, sv_ref, cgate_ref,
                  ga_ref, gb_ref, gc_ref, o_ref,
                  convw_ref, convb_ref, clng_ref, clnb_ref, slng_ref, slnb_ref, sguw_ref, sgub_ref,
                  wpa_ref, wpb_ref, wpc_ref, wo_ref, fng_ref, out_ref, zbuf, zsh, conv_scr,
                  *, t_rows, final):
    @pl.when(pl.program_id(1) == 0)
    def _():
        zbuf[0:HALO, :] = jnp.zeros((HALO, BRANCH_W), F32)

    a = aglu_ref[:, :BRANCH_W].astype(F32)
    g = aglu_ref[:, BRANCH_W:].astype(F32)
    zbuf[HALO:HALO + t_rows, :] = a * jax.nn.sigmoid(g)
    rc = 64
    n_sh = t_rows + HALO - SUBLANE
    for c in range(BRANCH_W // LANE):
        cs = slice(c * LANE, (c + 1) * LANE)
        for sh in range(1, SUBLANE):
            zsh[sh - 1] = zbuf[sh:sh + n_sh, cs]
        for r in range(t_rows // rc):
            acc = jnp.broadcast_to(convb_ref[:, cs], (rc, LANE))
            for j in range(CONV_K):
                off = HALO - (CONV_K - 1) + j
                sh = off % SUBLANE
                base = r * rc + off - sh
                if sh == 0:
                    src = zbuf[base:base + rc, cs]
                else:
                    src = zsh[sh - 1, base:base + rc, :]
                acc = acc + convw_ref[j:j + 1, cs] * src
            conv_scr[r * rc:(r + 1) * rc, cs] = acc
    zbuf[0:HALO, :] = zbuf[t_rows:t_rows + HALO, :]
    za = _silu(_layer_norm(conv_scr[...], clng_ref[...], clnb_ref[...]))
    za = za * _silu(agate_ref[...].astype(F32))
    ya = jnp.dot(za.astype(BF16), wpa_ref[...], preferred_element_type=F32)

    zb = o_ref[...].astype(F32) * _silu(bgate_ref[...].astype(F32))
    yb = jnp.dot(zb.astype(BF16), wpb_ref[...], preferred_element_type=F32)

    v = _layer_norm(sv_ref[...].astype(F32), slng_ref[...], slnb_ref[...]).astype(BF16)
    row = lax.broadcasted_iota(jnp.int32, (CHUNK, CHUNK), 0)
    col = lax.broadcasted_iota(jnp.int32, (CHUNK, CHUNK), 1)
    tril = col <= row
    ws = [jnp.where(tril, sguw_ref[gi], 0.0).astype(BF16) for gi in range(SGU_GROUPS)]
    bs = [jnp.broadcast_to(sgub_ref[:, gi:gi + 1], (CHUNK, LANE)) for gi in range(SGU_GROUPS)]
    rows = []
    for n in range(t_rows // CHUNK):
        blocks = []
        for gi in range(SGU_GROUPS):
            vb = v[n * CHUNK:(n + 1) * CHUNK, gi * LANE:(gi + 1) * LANE]
            blocks.append(jnp.dot(ws[gi], vb, preferred_element_type=F32) + bs[gi])
        rows.append(jnp.concatenate(blocks, axis=1))
    mixed = jnp.concatenate(rows, axis=0)
    zc = u_ref[...].astype(F32) * mixed * _silu(cgate_ref[...].astype(F32))
    yc = jnp.dot(zc.astype(BF16), wpc_ref[...], preferred_element_type=F32)

    ga = jax.nn.sigmoid(ga_ref[...].astype(F32))
    gb = jax.nn.sigmoid(gb_ref[...].astype(F32))
    gc = jax.nn.sigmoid(gc_ref[...].astype(F32))
    merged = ga * ya + gb * yb + gc * yc
    xn = x_ref[...] + jnp.dot(merged.astype(BF16), wo_ref[...], preferred_element_type=F32)
    if final:
        ms = jnp.mean(xn * xn, axis=-1, keepdims=True)
        xn = xn * lax.rsqrt(ms + EPS) * fng_ref[...]
    out_ref[...] = xn


def _merge(x3d, proj3d, o3d, conv_w, conv_b, cln_g, cln_b, sln_g, sln_b, sgu_w, sgu_bt,
           w_pa, w_pb, w_pc, w_o, fn_g, layer, final, t_rows=512):
    b, s, _ = x3d.shape

    def pcols(col, width):
        return pl.BlockSpec((None, t_rows, width), lambda bi, t: (bi, t, (col * LANE) // width))

    def lvec(n):
        return pl.BlockSpec((None, 1, n), lambda bi, t: (layer, 0, 0))

    def lmat(r, c):
        return pl.BlockSpec((None, r, c), lambda bi, t: (layer, 0, 0))

    return pl.pallas_call(
        functools.partial(_merge_kernel, t_rows=t_rows, final=final),
        grid=(b, s // t_rows),
        in_specs=[
            pl.BlockSpec((None, t_rows, D_MODEL), lambda bi, t: (bi, t, 0)),
            pcols(COL_AGLU, 2 * BRANCH_W),
            pcols(COL_AGATE, BRANCH_W),
            pcols(COL_BGATE, BRANCH_W),
            pcols(COL_U, BRANCH_W),
            pcols(COL_SV, BRANCH_W),
            pcols(COL_CGATE, BRANCH_W),
            pcols(COL_GATES, D_MODEL),
            pcols(COL_GATES + D_MODEL // LANE, D_MODEL),
            pcols(COL_GATES + 2 * D_MODEL // LANE, D_MODEL),
            pl.BlockSpec((None, t_rows, BRANCH_W), lambda bi, t: (bi, t, 0)),
            lmat(CONV_K, BRANCH_W),
            lvec(BRANCH_W), lvec(BRANCH_W), lvec(BRANCH_W), lvec(BRANCH_W), lvec(BRANCH_W),
            pl.BlockSpec((None, SGU_GROUPS, CHUNK, CHUNK), lambda bi, t: (layer, 0, 0, 0)),
            lmat(CHUNK, SGU_GROUPS),
            lmat(BRANCH_W, D_MODEL), lmat(BRANCH_W, D_MODEL), lmat(BRANCH_W, D_MODEL),
            lmat(D_MODEL, D_MODEL),
            pl.BlockSpec((1, D_MODEL), lambda bi, t: (0, 0)),
        ],
        out_specs=pl.BlockSpec((None, t_rows, D_MODEL), lambda bi, t: (bi, t, 0)),
        out_shape=jax.ShapeDtypeStruct((b, s, D_MODEL), F32),
        scratch_shapes=[pltpu.VMEM((t_rows + HALO, BRANCH_W), F32),
                        pltpu.VMEM((SUBLANE - 1, t_rows + HALO - SUBLANE, LANE), F32),
                        pltpu.VMEM((t_rows, BRANCH_W), F32)],
        compiler_params=pltpu.CompilerParams(
            dimension_semantics=("arbitrary", "arbitrary"), vmem_limit_bytes=VMEM_LIMIT),
        name="merge",
    )(x3d, proj3d, proj3d, proj3d, proj3d, proj3d, proj3d, proj3d, proj3d, proj3d, o3d,
      conv_w, conv_b, cln_g, cln_b, sln_g, sln_b, sgu_w, sgu_bt, w_pa, w_pb, w_pc, w_o, fn_g)


def kernel(x, norm_g, w_in, conv_w, conv_b, conv_ln_g, conv_ln_b, lam_q1, lam_k1, lam_q2, lam_k2,
           diff_norm_g, sgu_ln_g, sgu_ln_b, sgu_w, sgu_b, w_pa, w_pb, w_pc, w_o, final_norm_g):
    b, s, d = x.shape
    depth = w_in.shape[0]
    row3 = lambda p: p[:, None, :]
    w_in_b, w_pa_b, w_pb_b, w_pc_b, w_o_b = (w.astype(BF16) for w in (w_in, w_pa, w_pb, w_pc, w_o))
    norm_g3, conv_b3, cln_g3, cln_b3 = row3(norm_g), row3(conv_b), row3(conv_ln_g), row3(conv_ln_b)
    sln_g3, sln_b3, sub_g3 = row3(sgu_ln_g), row3(sgu_ln_b), row3(diff_norm_g)
    lq1, lk1, lq2, lk2 = row3(lam_q1), row3(lam_k1), row3(lam_q2), row3(lam_k2)
    sgu_bt = jnp.swapaxes(sgu_b, 1, 2)
    fn_g = final_norm_g[None, :]
    col_scale = jnp.ones((1, IN_COLS), F32).at[:, COL_Q * LANE:COL_K * LANE].set(Q_SCALE)

    for layer in range(depth):
        lambda_init = 0.8 - 0.6 * math.exp(-0.3 * layer)
        proj = _inproj(x.reshape(b * s, d), norm_g3, w_in_b, col_scale, layer).reshape(b, s, IN_COLS)
        o = _attention(proj, lq1, lk1, lq2, lk2, sub_g3, layer, lambda_init)
        x = _merge(x, proj, o, conv_w, conv_b3, cln_g3, cln_b3, sln_g3, sln_b3, sgu_w, sgu_bt,
                   w_pa_b, w_pb_b, w_pc_b, w_o_b, fn_g, layer, final=(layer == depth - 1))
    return x
```

```python
import functools
import math

import jax
import jax.numpy as jnp
from jax import lax
from jax.experimental import pallas as pl
from jax.experimental.pallas import tpu as pltpu

D_MODEL = 1024
BRANCH_W = D_MODEL // 2
CONV_K = 31
DA_HEADS = 4
DA_HEAD_DIM = 64
DA_V_DIM = 128
SGU_GROUPS = 4
CHUNK = 128
IN_COLS = 8192
EPS = 1e-6

LANE = 128
SUBLANE = 8
COL_AGLU = 0
COL_AGATE = 1024 // LANE
COL_Q = 1536 // LANE
COL_K = 2048 // LANE
COL_V = 2560 // LANE
COL_BGATE = 3072 // LANE
COL_U = 3584 // LANE
COL_SV = 4096 // LANE
COL_CGATE = 4608 // LANE
COL_GATES = 5120 // LANE

ONES_ROWS = 16
Q_SCALE = DA_HEAD_DIM ** -0.5 * math.log2(math.e)
HALO = 32
NEG = -1e30

F32 = jnp.float32
BF16 = jnp.bfloat16

VMEM_LIMIT = 56 * 1024 * 1024


def _inproj_kernel(x_ref, g_ref, w_ref, cs_ref, o_ref, h_scr):
    @pl.when(pl.program_id(1) == 0)
    def _():
        x = x_ref[...]
        ms = jnp.mean(x * x, axis=-1, keepdims=True)
        h_scr[...] = (x * lax.rsqrt(ms + EPS) * g_ref[...]).astype(BF16)

    acc = jnp.dot(h_scr[...], w_ref[...], preferred_element_type=F32)
    o_ref[...] = (acc * cs_ref[...]).astype(BF16)


def _inproj(x2d, g, w_bf16, col_scale, layer, tm=2048, tn=1024):
    m = x2d.shape[0]
    return pl.pallas_call(
        _inproj_kernel,
        grid=(m // tm, IN_COLS // tn),
        in_specs=[
            pl.BlockSpec((tm, D_MODEL), lambda i, j: (i, 0)),
            pl.BlockSpec((None, 1, D_MODEL), lambda i, j: (layer, 0, 0)),
            pl.BlockSpec((None, D_MODEL, tn), lambda i, j: (layer, 0, j)),
            pl.BlockSpec((1, tn), lambda i, j: (0, j)),
        ],
        out_specs=pl.BlockSpec((tm, tn), lambda i, j: (i, j)),
        out_shape=jax.ShapeDtypeStruct((m, IN_COLS), BF16),
        scratch_shapes=[pltpu.VMEM((tm, D_MODEL), BF16)],
        compiler_params=pltpu.CompilerParams(
            dimension_semantics=("arbitrary", "arbitrary"), vmem_limit_bytes=VMEM_LIMIT),
        name="inproj",
    )(x2d, g, w_bf16, col_scale)


def _attn_kernel(lq1_ref, lk1_ref, lq2_ref, lk2_ref, sg_ref, q_ref, k_ref, v_ref, o_ref,
                 vt_scr, acc_scr, m_scr, sa_scr, sb_scr, *, lambda_init, tq, tk):
    i = pl.program_id(1)
    n_kv = v_ref.shape[0] // tk
    n_chain = 2 * DA_HEADS
    n_diag = tq // tk

    @pl.when(i == 0)
    def _():
        ones = jnp.ones((ONES_ROWS, tk), BF16)

        def fill(c, carry):
            vblk = v_ref[pl.ds(pl.multiple_of(c * tk, tk), tk), :]
            vt = vblk.astype(F32).T.astype(BF16)
            for h in range(DA_HEADS):
                vt_scr[c, h, 0:DA_V_DIM, :] = vt[h * DA_V_DIM:(h + 1) * DA_V_DIM, :]
                vt_scr[c, h, DA_V_DIM:, :] = ones
            return carry
        lax.fori_loop(0, n_kv, fill, 0)

    lam = (jnp.exp(jnp.sum(lq1_ref[...] * lk1_ref[...], axis=-1, keepdims=True))
           - jnp.exp(jnp.sum(lq2_ref[...] * lk2_ref[...], axis=-1, keepdims=True))
           + lambda_init)

    qt = q_ref[...].astype(F32).T
    sub = lax.broadcasted_iota(jnp.int32, (LANE, tq), 0)
    qtz = []
    for h in range(DA_HEADS):
        qth = qt[h * LANE:(h + 1) * LANE, :]
        qtz.append(jnp.where(sub < DA_HEAD_DIM, qth, 0.0).astype(BF16))
        qtz.append(jnp.where(sub >= DA_HEAD_DIM, qth, 0.0).astype(BF16))

    rel = (lax.broadcasted_iota(jnp.int32, (tk, tq), 0)
           - lax.broadcasted_iota(jnp.int32, (tk, tq), 1))

    acc_scr[...] = jnp.zeros(acc_scr.shape, F32)
    m_scr[...] = jnp.full(m_scr.shape, NEG, F32)

    def scores_into(j, dst):
        kb = k_ref[pl.ds(pl.multiple_of(j * tk, tk), tk), :]
        for c in range(n_chain):
            dst[c] = jnp.dot(kb[:, (c // 2) * LANE:(c // 2 + 1) * LANE], qtz[c],
                             preferred_element_type=F32)

    def softmax_pv(j, src, diag):
        pts, alphas = [], []
        for c in range(n_chain):
            st = src[c] if diag is None else jnp.where(rel <= -diag * tk, src[c], NEG)
            m = m_scr[c]
            m_new = jnp.maximum(m, jnp.max(st, axis=0, keepdims=True))
            m_scr[c] = m_new
            alphas.append(jnp.exp2(m - m_new))
            pts.append(jnp.exp2(st - m_new).astype(BF16))
        for c in range(n_chain):
            acc_scr[c] = alphas[c] * acc_scr[c] + jnp.dot(
                vt_scr[j, c // 2], pts[c], preferred_element_type=F32)

    bufs = (sa_scr, sb_scr)
    scores_into(0, sa_scr)

    def pair(t, carry):
        j = 2 * t
        scores_into(j + 1, sb_scr)
        softmax_pv(j, sa_scr, None)
        scores_into(j + 2, sa_scr)
        softmax_pv(j + 1, sb_scr, None)
        return carry
    lax.fori_loop(0, (n_diag // 2) * i, pair, 0)
    first = n_diag * i
    for d in range(n_diag):
        if d + 1 < n_diag:
            scores_into(first + d + 1, bufs[(d + 1) % 2])
        softmax_pv(first + d, bufs[d % 2], d)

    def normalised(c):
        return acc_scr[c, 0:DA_V_DIM, :] / acc_scr[c, DA_V_DIM:DA_V_DIM + 1, :]

    for h in range(DA_HEADS):
        ot = normalised(2 * h) - lam * normalised(2 * h + 1)
        o = ot.T
        ms = jnp.mean(o * o, axis=-1, keepdims=True)
        o = o * lax.rsqrt(ms + EPS) * sg_ref[...] * (1.0 - lambda_init)
        o_ref[:, h * DA_V_DIM:(h + 1) * DA_V_DIM] = o.astype(o_ref.dtype)


def _attention(proj3d, lq1, lk1, lq2, lk2, sub_g, layer, lambda_init, tq=512, tk=256):
    b, s, _ = proj3d.shape
    assert tq % (2 * tk) == 0 and s % tq == 0
    vec = lambda n: pl.BlockSpec((None, 1, n), lambda bi, i: (layer, 0, 0))
    lanes_per_branch = BRANCH_W // LANE
    n_chain = 2 * DA_HEADS
    return pl.pallas_call(
        functools.partial(_attn_kernel, lambda_init=lambda_init, tq=tq, tk=tk),
        grid=(b, s // tq),
        in_specs=[
            vec(DA_HEAD_DIM), vec(DA_HEAD_DIM), vec(DA_HEAD_DIM), vec(DA_HEAD_DIM), vec(DA_V_DIM),
            pl.BlockSpec((None, tq, BRANCH_W), lambda bi, i: (bi, i, COL_Q // lanes_per_branch)),
            pl.BlockSpec((None, s, BRANCH_W), lambda bi, i: (bi, 0, COL_K // lanes_per_branch)),
            pl.BlockSpec((None, s, BRANCH_W), lambda bi, i: (bi, 0, COL_V // lanes_per_branch)),
        ],
        out_specs=pl.BlockSpec((None, tq, BRANCH_W), lambda bi, i: (bi, i, 0)),
        out_shape=jax.ShapeDtypeStruct((b, s, BRANCH_W), BF16),
        scratch_shapes=[pltpu.VMEM((s // tk, DA_HEADS, DA_V_DIM + ONES_ROWS, tk), BF16),
                        pltpu.VMEM((n_chain, DA_V_DIM + ONES_ROWS, tq), F32),
                        pltpu.VMEM((n_chain, 1, tq), F32),
                        pltpu.VMEM((n_chain, tk, tq), F32),
                        pltpu.VMEM((n_chain, tk, tq), F32)],
        compiler_params=pltpu.CompilerParams(
            dimension_semantics=("arbitrary", "arbitrary"),
            vmem_limit_bytes=VMEM_LIMIT),
        name="diff_attn",
    )(lq1, lk1, lq2, lk2, sub_g, proj3d, proj3d, proj3d)


def _layer_norm(v, g, b):
    mu = jnp.mean(v, axis=-1, keepdims=True)
    d = v - mu
    var = jnp.mean(d * d, axis=-1, keepdims=True)
    return d * lax.rsqrt(var + EPS) * g + b


def _silu(v):
    return v * jax.nn.sigmoid(v)


def _merge_kernel(x_ref, aglu_ref, agate_ref, bgate_ref, u_ref, sv_ref, cgate_ref,
                  ga_ref, gb_ref, gc_ref, o_ref,
                  convw_ref, convb_ref, clng_ref, clnb_ref, slng_ref, slnb_ref, sguw_ref, sgub_ref,
                  wpa_ref, wpb_ref, wpc_ref, wo_ref, fng_ref, out_ref, zbuf, zsh, conv_scr,
                  *, t_rows, final):
    @pl.when(pl.program_id(1) == 0)
    def _():
        zbuf[0:HALO, :] = jnp.zeros((HALO, BRANCH_W), F32)

    a = aglu_ref[:, :BRANCH_W].astype(F32)
    g = aglu_ref[:, BRANCH_W:].astype(F32)
    zbuf[HALO:HALO + t_rows, :] = a * jax.nn.sigmoid(g)
    rc = 64
    n_sh = t_rows + HALO - SUBLANE
    for c in range(BRANCH_W // LANE):
        cs = slice(c * LANE, (c + 1) * LANE)
        for sh in range(1, SUBLANE):
            zsh[sh - 1] = zbuf[sh:sh + n_sh, cs]
        for r in range(t_rows // rc):
            acc = jnp.broadcast_to(convb_ref[:, cs], (rc, LANE))
            for j in range(CONV_K):
                off = HALO - (CONV_K - 1) + j
                sh = off % SUBLANE
                base = r * rc + off - sh
                if sh == 0:
                    src = zbuf[base:base + rc, cs]
                else:
                    src = zsh[sh - 1, base:base + rc, :]
                acc = acc + convw_ref[j:j + 1, cs] * src
            conv_scr[r * rc:(r + 1) * rc, cs] = acc
    zbuf[0:HALO, :] = zbuf[t_rows:t_rows + HALO, :]
    za = _silu(_layer_norm(conv_scr[...], clng_ref[...], clnb_ref[...]))
    za = za * _silu(agate_ref[...].astype(F32))
    ya = jnp.dot(za.astype(BF16), wpa_ref[...], preferred_element_type=F32)

    zb = o_ref[...].astype(F32) * _silu(bgate_ref[...].astype(F32))
    yb = jnp.dot(zb.astype(BF16), wpb_ref[...], preferred_element_type=F32)

    v = _layer_norm(sv_ref[...].astype(F32), slng_ref[...], slnb_ref[...]).astype(BF16)
    row = lax.broadcasted_iota(jnp.int32, (CHUNK, CHUNK), 0)
    col = lax.broadcasted_iota(jnp.int32, (CHUNK, CHUNK), 1)
    tril = col <= row
    ws = [jnp.where(tril, sguw_ref[gi], 0.0).astype(BF16) for gi in range(SGU_GROUPS)]
    bs = [jnp.broadcast_to(sgub_ref[:, gi:gi + 1], (CHUNK, LANE)) for gi in range(SGU_GROUPS)]
    rows = []
    for n in range(t_rows // CHUNK):
        blocks = []
        for gi in range(SGU_GROUPS):
            vb = v[n * CHUNK:(n + 1) * CHUNK, gi * LANE:(gi + 1) * LANE]
            blocks.append(jnp.dot(ws[gi], vb, preferred_element_type=F32) + bs[gi])
        rows.append(jnp.concatenate(blocks, axis=1))
    mixed = jnp.concatenate(rows, axis=0)
    zc = u_ref[...].astype(F32) * mixed * _silu(cgate_ref[...].astype(F32))
    yc = jnp.dot(zc.astype(BF16), wpc_ref[...], preferred_element_type=F32)

    ga = jax.nn.sigmoid(ga_ref[...].astype(F32))
    gb = jax.nn.sigmoid(gb_ref[...].astype(F32))
    gc = jax.nn.sigmoid(gc_ref[...].astype(F32))
    merged = ga * ya + gb * yb + gc * yc
    xn = x_ref[...] + jnp.dot(merged.astype(BF16), wo_ref[...], preferred_element_type=F32)
    if final:
        ms = jnp.mean(xn * xn, axis=-1, keepdims=True)
        xn = xn * lax.rsqrt(ms + EPS) * fng_ref[...]
    out_ref[...] = xn


def _merge(x3d, proj3d, o3d, conv_w, conv_b, cln_g, cln_b, sln_g, sln_b, sgu_w, sgu_bt,
           w_pa, w_pb, w_pc, w_o, fn_g, layer, final, t_rows=512):
    b, s, _ = x3d.shape

    def pcols(col, width):
        return pl.BlockSpec((None, t_rows, width), lambda bi, t: (bi, t, (col * LANE) // width))

    def lvec(n):
        return pl.BlockSpec((None, 1, n), lambda bi, t: (layer, 0, 0))

    def lmat(r, c):
        return pl.BlockSpec((None, r, c), lambda bi, t: (layer, 0, 0))

    return pl.pallas_call(
        functools.partial(_merge_kernel, t_rows=t_rows, final=final),
        grid=(b, s // t_rows),
        in_specs=[
            pl.BlockSpec((None, t_rows, D_MODEL), lambda bi, t: (bi, t, 0)),
            pcols(COL_AGLU, 2 * BRANCH_W),
            pcols(COL_AGATE, BRANCH_W),
            pcols(COL_BGATE, BRANCH_W),
            pcols(COL_U, BRANCH_W),
            pcols(COL_SV, BRANCH_W),
            pcols(COL_CGATE, BRANCH_W),
            pcols(COL_GATES, D_MODEL),
            pcols(COL_GATES + D_MODEL // LANE, D_MODEL),
            pcols(COL_GATES + 2 * D_MODEL // LANE, D_MODEL),
            pl.BlockSpec((None, t_rows, BRANCH_W), lambda bi, t: (bi, t, 0)),
            lmat(CONV_K, BRANCH_W),
            lvec(BRANCH_W), lvec(BRANCH_W), lvec(BRANCH_W), lvec(BRANCH_W), lvec(BRANCH_W),
            pl.BlockSpec((None, SGU_GROUPS, CHUNK, CHUNK), lambda bi, t: (layer, 0, 0, 0)),
            lmat(CHUNK, SGU_GROUPS),
            lmat(BRANCH_W, D_MODEL), lmat(BRANCH_W, D_MODEL), lmat(BRANCH_W, D_MODEL),
            lmat(D_MODEL, D_MODEL),
            pl.BlockSpec((1, D_MODEL), lambda bi, t: (0, 0)),
        ],
        out_specs=pl.BlockSpec((None, t_rows, D_MODEL), lambda bi, t: (bi, t, 0)),
        out_shape=jax.ShapeDtypeStruct((b, s, D_MODEL), F32),
        scratch_shapes=[pltpu.VMEM((t_rows + HALO, BRANCH_W), F32),
                        pltpu.VMEM((SUBLANE - 1, t_rows + HALO - SUBLANE, LANE), F32),
                        pltpu.VMEM((t_rows, BRANCH_W), F32)],
        compiler_params=pltpu.CompilerParams(
            dimension_semantics=("arbitrary", "arbitrary"), vmem_limit_bytes=VMEM_LIMIT),
        name="merge",
    )(x3d, proj3d, proj3d, proj3d, proj3d, proj3d, proj3d, proj3d, proj3d, proj3d, o3d,
      conv_w, conv_b, cln_g, cln_b, sln_g, sln_b, sgu_w, sgu_bt, w_pa, w_pb, w_pc, w_o, fn_g)


def kernel(x, norm_g, w_in, conv_w, conv_b, conv_ln_g, conv_ln_b, lam_q1, lam_k1, lam_q2, lam_k2,
           diff_norm_g, sgu_ln_g, sgu_ln_b, sgu_w, sgu_b, w_pa, w_pb, w_pc, w_o, final_norm_g):
    b, s, d = x.shape
    depth = w_in.shape[0]
    row3 = lambda p: p[:, None, :]
    w_in_b, w_pa_b, w_pb_b, w_pc_b, w_o_b = (w.astype(BF16) for w in (w_in, w_pa, w_pb, w_pc, w_o))
    norm_g3, conv_b3, cln_g3, cln_b3 = row3(norm_g), row3(conv_b), row3(conv_ln_g), row3(conv_ln_b)
    sln_g3, sln_b3, sub_g3 = row3(sgu_ln_g), row3(sgu_ln_b), row3(diff_norm_g)
    lq1, lk1, lq2, lk2 = row3(lam_q1), row3(lam_k1), row3(lam_q2), row3(lam_k2)
    sgu_bt = jnp.swapaxes(sgu_b, 1, 2)
    fn_g = final_norm_g[None, :]
    col_scale = jnp.ones((1, IN_COLS), F32).at[:, COL_Q * LANE:COL_K * LANE].set(Q_SCALE)

    for layer in range(depth):
        lambda_init = 0.8 - 0.6 * math.exp(-0.3 * layer)
        proj = _inproj(x.reshape(b * s, d), norm_g3, w_in_b, col_scale, layer).reshape(b, s, IN_COLS)
        o = _attention(proj, lq1, lk1, lq2, lk2, sub_g3, layer, lambda_init)
        x = _merge(x, proj, o, conv_w, conv_b3, cln_g3, cln_b3, sln_g3, sln_b3, sgu_w, sgu_bt,
                   w_pa_b, w_pb_b, w_pc_b, w_o_b, fn_g, layer, final=(layer == depth - 1))
    return x
```

```python
import functools
import math

import jax
import jax.numpy as jnp
from jax import lax
from jax.experimental import pallas as pl
from jax.experimental.pallas import tpu as pltpu

D_MODEL = 1024
BRANCH_W = D_MODEL // 2
CONV_K = 31
DA_HEADS = 4
DA_HEAD_DIM = 64
DA_V_DIM = 128
SGU_GROUPS = 4
CHUNK = 128
IN_COLS = 8192
EPS = 1e-6

LANE = 128
SUBLANE = 8
GLU_COLS = 2 * BRANCH_W
PROJ_COLS = IN_COLS - GLU_COLS
COL_AGATE = (1024 - GLU_COLS) // LANE
COL_Q = (1536 - GLU_COLS) // LANE
COL_K = (2048 - GLU_COLS) // LANE
COL_V = (2560 - GLU_COLS) // LANE
COL_BGATE = (3072 - GLU_COLS) // LANE
COL_U = (3584 - GLU_COLS) // LANE
COL_SV = (4096 - GLU_COLS) // LANE
COL_CGATE = (4608 - GLU_COLS) // LANE
COL_GATES = (5120 - GLU_COLS) // LANE

ONES_ROWS = 16
Q_SCALE = DA_HEAD_DIM ** -0.5 * math.log2(math.e)
HALO = 32
NEG = -1e30

F32 = jnp.float32
BF16 = jnp.bfloat16

VMEM_LIMIT = 56 * 1024 * 1024


CONV_RC = 64
CONV_UNITS = 5
N_PIECE = 4
N_ROWBLK = 8
USH_SLOTS = 8


def _inproj_kernel(x_ref, g_ref, w_ref, cs_ref, cw_ref, cb_ref, o_ref, conv_ref,
                   h_scr, zbuf, ush, *, tm, tn, seq_tiles):
    i = pl.program_id(0)
    j = pl.program_id(1)
    n_lb = BRANCH_W // LANE
    n_chunk = tm // CONV_RC
    span = CONV_RC + HALO

    @pl.when(j == 0)
    def _():
        x = x_ref[...]
        ms = jnp.mean(x * x, axis=-1, keepdims=True)
        h_scr[...] = (x * lax.rsqrt(ms + EPS) * g_ref[...]).astype(BF16)

    @pl.when((i == 0) & (j == 0))
    def _():
        zbuf[...] = jnp.zeros(zbuf.shape, F32)

    def conv_unit(r, c, slot):
        r0 = pl.multiple_of(r * CONV_RC, CONV_RC)
        blk = zbuf[c, pl.ds(r0, span), :]
        for s in range(1, SUBLANE):
            ush[slot, s - 1, SUBLANE - s:SUBLANE - s + span, :] = blk
        cs = slice(c * LANE, (c + 1) * LANE)
        acc = jnp.broadcast_to(cb_ref[:, cs], (CONV_RC, LANE))
        for t in range(CONV_K):
            off = HALO - (CONV_K - 1) + t
            s = off % SUBLANE
            base = off - s
            if s == 0:
                src = blk[base:base + CONV_RC, :]
            else:
                src = ush[slot, s - 1, SUBLANE + base:SUBLANE + base + CONV_RC, :]
            acc = acc + cw_ref[t:t + 1, cs] * src
        conv_ref[pl.ds(r0, CONV_RC), cs] = acc

    pw = tn // N_PIECE
    rb = tm // N_ROWBLK
    units = [(u, c) for u in range(CONV_UNITS) for c in range(n_lb)]
    k = 0
    for p in range(N_PIECE):
        ps = slice(p * pw, (p + 1) * pw)
        for q in range(N_ROWBLK):
            rs = slice(q * rb, (q + 1) * rb)
            acc = jnp.dot(h_scr[rs, :], w_ref[:, ps], preferred_element_type=F32)
            o_ref[rs, ps] = (acc * cs_ref[:, ps]).astype(BF16)
            if k < len(units):
                u, c = units[k]
                r = jnp.clip((j - 1) * CONV_UNITS + u, 0, n_chunk - 1)
                conv_unit(r, c, k % USH_SLOTS)
                k += 1

    @pl.when(j == 0)
    def _():
        a = o_ref[:, :BRANCH_W].astype(F32)
        gl = o_ref[:, BRANCH_W:].astype(F32)
        z = a * jax.nn.sigmoid(gl)
        seq_start = lax.rem(i, seq_tiles) == 0
        for c in range(n_lb):
            zbuf[c, 0:HALO, :] = jnp.where(seq_start, 0.0, zbuf[c, tm:tm + HALO, :])
            zbuf[c, HALO:HALO + tm, :] = z[:, c * LANE:(c + 1) * LANE]


def _inproj(x2d, g, w_bf16, col_scale, conv_w, conv_b, layer, seq, tm=2048, tn=1024):
    m = x2d.shape[0]
    n_units = CONV_UNITS * (BRANCH_W // LANE)
    assert GLU_COLS == tn and seq % tm == 0 and n_units <= N_PIECE * N_ROWBLK
    assert (IN_COLS // tn - 1) * CONV_UNITS >= tm // CONV_RC
    return pl.pallas_call(
        functools.partial(_inproj_kernel, tm=tm, tn=tn, seq_tiles=seq // tm),
        grid=(m // tm, IN_COLS // tn),
        in_specs=[
            pl.BlockSpec((tm, D_MODEL), lambda i, j: (i, 0)),
            pl.BlockSpec((None, 1, D_MODEL), lambda i, j: (layer, 0, 0)),
            pl.BlockSpec((None, D_MODEL, tn), lambda i, j: (layer, 0, j)),
            pl.BlockSpec((1, tn), lambda i, j: (0, j)),
            pl.BlockSpec((None, CONV_K, BRANCH_W), lambda i, j: (layer, 0, 0)),
            pl.BlockSpec((None, 1, BRANCH_W), lambda i, j: (layer, 0, 0)),
        ],
        out_specs=[pl.BlockSpec((tm, tn), lambda i, j: (i, jnp.maximum(j - 1, 0))),
                   pl.BlockSpec((tm, BRANCH_W), lambda i, j: (i, 0))],
        out_shape=[jax.ShapeDtypeStruct((m, PROJ_COLS), BF16),
                   jax.ShapeDtypeStruct((m, BRANCH_W), F32)],
        scratch_shapes=[pltpu.VMEM((tm, D_MODEL), BF16),
                        pltpu.VMEM((BRANCH_W // LANE, tm + HALO, LANE), F32),
                        pltpu.VMEM((USH_SLOTS, SUBLANE - 1, SUBLANE + CONV_RC + HALO, LANE), F32)],
        compiler_params=pltpu.CompilerParams(
            dimension_semantics=("arbitrary", "arbitrary"), vmem_limit_bytes=VMEM_LIMIT),
        name="inproj",
    )(x2d, g, w_bf16, col_scale, conv_w, conv_b)


def _attn_kernel(lq1_ref, lk1_ref, lq2_ref, lk2_ref, sg_ref, q_ref, k_ref, v_ref, o_ref,
                 vt_scr, acc_scr, m_scr, sa_scr, sb_scr, *, lambda_init, tq, tk):
    i = pl.program_id(1)
    n_kv = v_ref.shape[0] // tk
    n_chain = 2 * DA_HEADS
    n_diag = tq // tk

    @pl.when(i == 0)
    def _():
        ones = jnp.ones((ONES_ROWS, tk), BF16)

        def fill(c, carry):
            vblk = v_ref[pl.ds(pl.multiple_of(c * tk, tk), tk), :]
            vt = vblk.astype(F32).T.astype(BF16)
            for h in range(DA_HEADS):
                vt_scr[c, h, 0:DA_V_DIM, :] = vt[h * DA_V_DIM:(h + 1) * DA_V_DIM, :]
                vt_scr[c, h, DA_V_DIM:, :] = ones
            return carry
        lax.fori_loop(0, n_kv, fill, 0)

    lam = (jnp.exp(jnp.sum(lq1_ref[...] * lk1_ref[...], axis=-1, keepdims=True))
           - jnp.exp(jnp.sum(lq2_ref[...] * lk2_ref[...], axis=-1, keepdims=True))
           + lambda_init)

    qt = q_ref[...].astype(F32).T
    sub = lax.broadcasted_iota(jnp.int32, (LANE, tq), 0)
    qtz = []
    for h in range(DA_HEADS):
        qth = qt[h * LANE:(h + 1) * LANE, :]
        qtz.append(jnp.where(sub < DA_HEAD_DIM, qth, 0.0).astype(BF16))
        qtz.append(jnp.where(sub >= DA_HEAD_DIM, qth, 0.0).astype(BF16))

    rel = (lax.broadcasted_iota(jnp.int32, (tk, tq), 0)
           - lax.broadcasted_iota(jnp.int32, (tk, tq), 1))

    acc_scr[...] = jnp.zeros(acc_scr.shape, F32)
    m_scr[...] = jnp.full(m_scr.shape, NEG, F32)

    def scores_into(j, dst):
        kb = k_ref[pl.ds(pl.multiple_of(j * tk, tk), tk), :]
        for c in range(n_chain):
            dst[c] = jnp.dot(kb[:, (c // 2) * LANE:(c // 2 + 1) * LANE], qtz[c],
                             preferred_element_type=F32)

    def softmax_pv(j, src, diag):
        pts, alphas = [], []
        for c in range(n_chain):
            st = src[c] if diag is None else jnp.where(rel <= -diag * tk, src[c], NEG)
            m = m_scr[c]
            m_new = jnp.maximum(m, jnp.max(st, axis=0, keepdims=True))
            m_scr[c] = m_new
            alphas.append(jnp.exp2(m - m_new))
            pts.append(jnp.exp2(st - m_new).astype(BF16))
        for c in range(n_chain):
            acc_scr[c] = alphas[c] * acc_scr[c] + jnp.dot(
                vt_scr[j, c // 2], pts[c], preferred_element_type=F32)

    bufs = (sa_scr, sb_scr)
    scores_into(0, sa_scr)

    def pair(t, carry):
        j = 2 * t
        scores_into(j + 1, sb_scr)
        softmax_pv(j, sa_scr, None)
        scores_into(j + 2, sa_scr)
        softmax_pv(j + 1, sb_scr, None)
        return carry
    lax.fori_loop(0, (n_diag // 2) * i, pair, 0)
    first = n_diag * i
    for d in range(n_diag):
        if d + 1 < n_diag:
            scores_into(first + d + 1, bufs[(d + 1) % 2])
        softmax_pv(first + d, bufs[d % 2], d)

    def normalised(c):
        return acc_scr[c, 0:DA_V_DIM, :] / acc_scr[c, DA_V_DIM:DA_V_DIM + 1, :]

    for h in range(DA_HEADS):
        ot = normalised(2 * h) - lam * normalised(2 * h + 1)
        o = ot.T
        ms = jnp.mean(o * o, axis=-1, keepdims=True)
        o = o * lax.rsqrt(ms + EPS) * sg_ref[...] * (1.0 - lambda_init)
        o_ref[:, h * DA_V_DIM:(h + 1) * DA_V_DIM] = o.astype(o_ref.dtype)


def _attention(proj3d, lq1, lk1, lq2, lk2, sub_g, layer, lambda_init, tq=512, tk=256):
    b, s, _ = proj3d.shape
    assert tq % (2 * tk) == 0 and s % tq == 0
    vec = lambda n: pl.BlockSpec((None, 1, n), lambda bi, i: (layer, 0, 0))
    lanes_per_branch = BRANCH_W // LANE
    n_chain = 2 * DA_HEADS
    return pl.pallas_call(
        functools.partial(_attn_kernel, lambda_init=lambda_init, tq=tq, tk=tk),
        grid=(b, s // tq),
        in_specs=[
            vec(DA_HEAD_DIM), vec(DA_HEAD_DIM), vec(DA_HEAD_DIM), vec(DA_HEAD_DIM), vec(DA_V_DIM),
            pl.BlockSpec((None, tq, BRANCH_W), lambda bi, i: (bi, i, COL_Q // lanes_per_branch)),
            pl.BlockSpec((None, s, BRANCH_W), lambda bi, i: (bi, 0, COL_K // lanes_per_branch)),
            pl.BlockSpec((None, s, BRANCH_W), lambda bi, i: (bi, 0, COL_V // lanes_per_branch)),
        ],
        out_specs=pl.BlockSpec((None, tq, BRANCH_W), lambda bi, i: (bi, i, 0)),
        out_shape=jax.ShapeDtypeStruct((b, s, BRANCH_W), BF16),
        scratch_shapes=[pltpu.VMEM((s // tk, DA_HEADS, DA_V_DIM + ONES_ROWS, tk), BF16),
                        pltpu.VMEM((n_chain, DA_V_DIM + ONES_ROWS, tq), F32),
                        pltpu.VMEM((n_chain, 1, tq), F32),
                        pltpu.VMEM((n_chain, tk, tq), F32),
                        pltpu.VMEM((n_chain, tk, tq), F32)],
        compiler_params=pltpu.CompilerParams(
            dimension_semantics=("arbitrary", "arbitrary"),
            vmem_limit_bytes=VMEM_LIMIT),
        name="diff_attn",
    )(lq1, lk1, lq2, lk2, sub_g, proj3d, proj3d, proj3d)


def _layer_norm(v, g, b):
    mu = jnp.mean(v, axis=-1, keepdims=True)
    d = v - mu
    var = jnp.mean(d * d, axis=-1, keepdims=True)
    return d * lax.rsqrt(var + EPS) * g + b


def _silu(v):
    return v * jax.nn.sigmoid(v)


def _merge_kernel(x_ref, conv_ref, agate_ref, bgate_ref, u_ref, sv_ref, cgate_ref,
                  ga_ref, gb_ref, gc_ref, o_ref,
                  clng_ref, clnb_ref, slng_ref, slnb_ref, sguw_ref, sgub_ref,
                  wpa_ref, wpb_ref, wpc_ref, wo_ref, fng_ref, out_ref, *, t_rows, final):
    zb = o_ref[...] * _silu(bgate_ref[...])
    yb = jnp.dot(zb, wpb_ref[...], preferred_element_type=F32)

    v = _layer_norm(sv_ref[...].astype(F32), slng_ref[...], slnb_ref[...]).astype(BF16)
    row = lax.broadcasted_iota(jnp.int32, (CHUNK, CHUNK), 0)
    col = lax.broadcasted_iota(jnp.int32, (CHUNK, CHUNK), 1)
    tril = col <= row
    ws = [jnp.where(tril, sguw_ref[gi], 0.0).astype(BF16) for gi in range(SGU_GROUPS)]
    bs = [jnp.broadcast_to(sgub_ref[:, gi:gi + 1], (CHUNK, LANE)) for gi in range(SGU_GROUPS)]
    rows = []
    for n in range(t_rows // CHUNK):
        blocks = []
        for gi in range(SGU_GROUPS):
            vb = v[n * CHUNK:(n + 1) * CHUNK, gi * LANE:(gi + 1) * LANE]
            blocks.append(jnp.dot(ws[gi], vb, preferred_element_type=F32) + bs[gi])
        rows.append(jnp.concatenate(blocks, axis=1))
    mixed = jnp.concatenate(rows, axis=0)
    zc = mixed.astype(BF16) * (u_ref[...] * _silu(cgate_ref[...]))
    yc = jnp.dot(zc, wpc_ref[...], preferred_element_type=F32)

    za = _silu(_layer_norm(conv_ref[...], clng_ref[...], clnb_ref[...]))
    za = za.astype(BF16) * _silu(agate_ref[...])
    ya = jnp.dot(za, wpa_ref[...], preferred_element_type=F32)

    ga = jax.nn.sigmoid(ga_ref[...].astype(F32))
    gb = jax.nn.sigmoid(gb_ref[...].astype(F32))
    gc = jax.nn.sigmoid(gc_ref[...].astype(F32))
    merged = ga * ya + gb * yb + gc * yc
    xn = x_ref[...] + jnp.dot(merged.astype(BF16), wo_ref[...], preferred_element_type=F32)
    if final:
        ms = jnp.mean(xn * xn, axis=-1, keepdims=True)
        xn = xn * lax.rsqrt(ms + EPS) * fng_ref[...]
    out_ref[...] = xn


def _merge(x3d, proj3d, conv3d, o3d, cln_g, cln_b, sln_g, sln_b, sgu_w, sgu_bt,
           w_pa, w_pb, w_pc, w_o, fn_g, layer, final, t_rows=512):
    b, s, _ = x3d.shape

    def pcols(col, width):
        return pl.BlockSpec((None, t_rows, width), lambda bi, t: (bi, t, (col * LANE) // width))

    def rows(width):
        return pl.BlockSpec((None, t_rows, width), lambda bi, t: (bi, t, 0))

    def lvec(n):
        return pl.BlockSpec((None, 1, n), lambda bi, t: (layer, 0, 0))

    def lmat(r, c):
        return pl.BlockSpec((None, r, c), lambda bi, t: (layer, 0, 0))

    return pl.pallas_call(
        functools.partial(_merge_kernel, t_rows=t_rows, final=final),
        grid=(b, s // t_rows),
        in_specs=[
            rows(D_MODEL),
            rows(BRANCH_W),
            pcols(COL_AGATE, BRANCH_W),
            pcols(COL_BGATE, BRANCH_W),
            pcols(COL_U, BRANCH_W),
            pcols(COL_SV, BRANCH_W),
            pcols(COL_CGATE, BRANCH_W),
            pcols(COL_GATES, D_MODEL),
            pcols(COL_GATES + D_MODEL // LANE, D_MODEL),
            pcols(COL_GATES + 2 * D_MODEL // LANE, D_MODEL),
            rows(BRANCH_W),
            lvec(BRANCH_W), lvec(BRANCH_W), lvec(BRANCH_W), lvec(BRANCH_W),
            pl.BlockSpec((None, SGU_GROUPS, CHUNK, CHUNK), lambda bi, t: (layer, 0, 0, 0)),
            lmat(CHUNK, SGU_GROUPS),
            lmat(BRANCH_W, D_MODEL), lmat(BRANCH_W, D_MODEL), lmat(BRANCH_W, D_MODEL),
            lmat(D_MODEL, D_MODEL),
            pl.BlockSpec((1, D_MODEL), lambda bi, t: (0, 0)),
        ],
        out_specs=rows(D_MODEL),
        out_shape=jax.ShapeDtypeStruct((b, s, D_MODEL), F32),
        compiler_params=pltpu.CompilerParams(
            dimension_semantics=("arbitrary", "arbitrary"), vmem_limit_bytes=VMEM_LIMIT),
        name="merge",
    )(x3d, conv3d, proj3d, proj3d, proj3d, proj3d, proj3d, proj3d, proj3d, proj3d, o3d,
      cln_g, cln_b, sln_g, sln_b, sgu_w, sgu_bt, w_pa, w_pb, w_pc, w_o, fn_g)


def kernel(x, norm_g, w_in, conv_w, conv_b, conv_ln_g, conv_ln_b, lam_q1, lam_k1, lam_q2, lam_k2,
           diff_norm_g, sgu_ln_g, sgu_ln_b, sgu_w, sgu_b, w_pa, w_pb, w_pc, w_o, final_norm_g):
    b, s, d = x.shape
    depth = w_in.shape[0]
    row3 = lambda p: p[:, None, :]
    w_in_b, w_pa_b, w_pb_b, w_pc_b, w_o_b = (w.astype(BF16) for w in (w_in, w_pa, w_pb, w_pc, w_o))
    norm_g3, conv_b3, cln_g3, cln_b3 = row3(norm_g), row3(conv_b), row3(conv_ln_g), row3(conv_ln_b)
    sln_g3, sln_b3, sub_g3 = row3(sgu_ln_g), row3(sgu_ln_b), row3(diff_norm_g)
    lq1, lk1, lq2, lk2 = row3(lam_q1), row3(lam_k1), row3(lam_q2), row3(lam_k2)
    sgu_bt = jnp.swapaxes(sgu_b, 1, 2)
    fn_g = final_norm_g[None, :]
    q_lo = GLU_COLS + COL_Q * LANE
    col_scale = jnp.ones((1, IN_COLS), F32).at[:, q_lo:q_lo + BRANCH_W].set(Q_SCALE)

    for layer in range(depth):
        lambda_init = 0.8 - 0.6 * math.exp(-0.3 * layer)
        proj, conv = _inproj(x.reshape(b * s, d), norm_g3, w_in_b, col_scale, conv_w, conv_b3,
                             layer, s)
        proj = proj.reshape(b, s, PROJ_COLS)
        o = _attention(proj, lq1, lk1, lq2, lk2, sub_g3, layer, lambda_init)
        x = _merge(x, proj, conv.reshape(b, s, BRANCH_W), o, cln_g3, cln_b3, sln_g3, sln_b3,
                   sgu_w, sgu_bt, w_pa_b, w_pb_b, w_pc_b, w_o_b, fn_g, layer,
                   final=(layer == depth - 1))
    return x
```

```python
import functools
import math

import jax
import jax.numpy as jnp
from jax import lax
from jax.experimental import pallas as pl
from jax.experimental.pallas import tpu as pltpu

D_MODEL = 1024
BRANCH_W = D_MODEL // 2
CONV_K = 31
DA_HEADS = 4
DA_HEAD_DIM = 64
DA_V_DIM = 128
SGU_GROUPS = 4
CHUNK = 128
IN_COLS = 8192
EPS = 1e-6

LANE = 128
SUBLANE = 8
COL_AGLU = 0
COL_AGATE = 1024 // LANE
COL_Q = 1536 // LANE
COL_K = 2048 // LANE
COL_V = 2560 // LANE
COL_BGATE = 3072 // LANE
COL_U = 3584 // LANE
COL_SV = 4096 // LANE
COL_CGATE = 4608 // LANE
COL_GATES = 5120 // LANE

ONES_ROWS = 16
Q_SCALE = DA_HEAD_DIM ** -0.5 * math.log2(math.e)
HALO = 32
NEG = -1e30

F32 = jnp.float32
BF16 = jnp.bfloat16

VMEM_LIMIT = 56 * 1024 * 1024


def _inproj_kernel(x_ref, g_ref, w_ref, cs_ref, o_ref, h_scr):
    @pl.when(pl.program_id(1) == 0)
    def _():
        x = x_ref[...]
        ms = jnp.mean(x * x, axis=-1, keepdims=True)
        h_scr[...] = (x * lax.rsqrt(ms + EPS) * g_ref[...]).astype(BF16)

    acc = jnp.dot(h_scr[...], w_ref[...], preferred_element_type=F32)
    o_ref[...] = (acc * cs_ref[...]).astype(BF16)


def _inproj(x2d, g, w_bf16, col_scale, layer, tm=2048, tn=2048):
    m = x2d.shape[0]
    return pl.pallas_call(
        _inproj_kernel,
        grid=(m // tm, IN_COLS // tn),
        in_specs=[
            pl.BlockSpec((tm, D_MODEL), lambda i, j: (i, 0)),
            pl.BlockSpec((None, 1, D_MODEL), lambda i, j: (layer, 0, 0)),
            pl.BlockSpec((None, D_MODEL, tn), lambda i, j: (layer, 0, j)),
            pl.BlockSpec((1, tn), lambda i, j: (0, j)),
        ],
        out_specs=pl.BlockSpec((tm, tn), lambda i, j: (i, j)),
        out_shape=jax.ShapeDtypeStruct((m, IN_COLS), BF16),
        scratch_shapes=[pltpu.VMEM((tm, D_MODEL), BF16)],
        compiler_params=pltpu.CompilerParams(
            dimension_semantics=("arbitrary", "arbitrary"), vmem_limit_bytes=VMEM_LIMIT),
        name="inproj",
    )(x2d, g, w_bf16, col_scale)


def _attn_kernel(lq1_ref, lk1_ref, lq2_ref, lk2_ref, sg_ref, q_ref, k_ref, v_ref, o_ref,
                 vt_scr, acc_scr, m_scr, sa_scr, sb_scr, *, lambda_init, tq, tk):
    i = pl.program_id(1)
    n_kv = v_ref.shape[0] // tk
    n_chain = 2 * DA_HEADS
    n_diag = tq // tk

    @pl.when(i == 0)
    def _():
        ones = jnp.ones((ONES_ROWS, tk), BF16)

        def fill(c, carry):
            vblk = v_ref[pl.ds(pl.multiple_of(c * tk, tk), tk), :]
            vt = vblk.astype(F32).T.astype(BF16)
            for h in range(DA_HEADS):
                vt_scr[c, h, 0:DA_V_DIM, :] = vt[h * DA_V_DIM:(h + 1) * DA_V_DIM, :]
                vt_scr[c, h, DA_V_DIM:, :] = ones
            return carry
        lax.fori_loop(0, n_kv, fill, 0)

    lam = (jnp.exp(jnp.sum(lq1_ref[...] * lk1_ref[...], axis=-1, keepdims=True))
           - jnp.exp(jnp.sum(lq2_ref[...] * lk2_ref[...], axis=-1, keepdims=True))
           + lambda_init)

    qt = q_ref[...].astype(F32).T
    sub = lax.broadcasted_iota(jnp.int32, (LANE, tq), 0)
    qtz = []
    for h in range(DA_HEADS):
        qth = qt[h * LANE:(h + 1) * LANE, :]
        qtz.append(jnp.where(sub < DA_HEAD_DIM, qth, 0.0).astype(BF16))
        qtz.append(jnp.where(sub >= DA_HEAD_DIM, qth, 0.0).astype(BF16))

    rel = (lax.broadcasted_iota(jnp.int32, (tk, tq), 0)
           - lax.broadcasted_iota(jnp.int32, (tk, tq), 1))

    acc_scr[...] = jnp.zeros(acc_scr.shape, F32)
    m_scr[...] = jnp.full(m_scr.shape, NEG, F32)

    def scores_into(j, dst, q0=0):
        kb = k_ref[pl.ds(pl.multiple_of(j * tk, tk), tk), :]
        for c in range(n_chain):
            dst[c, :, q0:] = jnp.dot(kb[:, (c // 2) * LANE:(c // 2 + 1) * LANE], qtz[c][:, q0:],
                                     preferred_element_type=F32)

    def softmax_pv(j, src, diag):
        q0 = 0 if diag is None else diag * tk
        pts, alphas = [], []
        for c in range(n_chain):
            st = src[c, :, q0:]
            if diag is not None:
                st = jnp.where(rel[:, q0:] <= -diag * tk, st, NEG)
            m = m_scr[c, :, q0:]
            m_new = jnp.maximum(m, jnp.max(st, axis=0, keepdims=True))
            m_scr[c, :, q0:] = m_new
            alphas.append(jnp.exp2(m - m_new))
            pts.append(jnp.exp2(st - m_new).astype(BF16))
        for c in range(n_chain):
            acc_scr[c, :, q0:] = alphas[c] * acc_scr[c, :, q0:] + jnp.dot(
                vt_scr[j, c // 2], pts[c], preferred_element_type=F32)

    bufs = (sa_scr, sb_scr)
    scores_into(0, sa_scr)

    def pair(t, carry):
        j = 2 * t
        scores_into(j + 1, sb_scr)
        softmax_pv(j, sa_scr, None)
        scores_into(j + 2, sa_scr)
        softmax_pv(j + 1, sb_scr, None)
        return carry
    lax.fori_loop(0, (n_diag // 2) * i, pair, 0)
    first = n_diag * i
    for d in range(n_diag):
        if d + 1 < n_diag:
            scores_into(first + d + 1, bufs[(d + 1) % 2], (d + 1) * tk)
        softmax_pv(first + d, bufs[d % 2], d)

    def normalised(c):
        return acc_scr[c, 0:DA_V_DIM, :] * (1.0 / acc_scr[c, DA_V_DIM:DA_V_DIM + 1, :])

    for h in range(DA_HEADS):
        ot = normalised(2 * h) - lam * normalised(2 * h + 1)
        o = ot.T
        ms = jnp.mean(o * o, axis=-1, keepdims=True)
        o = o * lax.rsqrt(ms + EPS) * sg_ref[...] * (1.0 - lambda_init)
        o_ref[:, h * DA_V_DIM:(h + 1) * DA_V_DIM] = o.astype(o_ref.dtype)


def _attention(proj3d, lq1, lk1, lq2, lk2, sub_g, layer, lambda_init, tq=512, tk=256):
    b, s, _ = proj3d.shape
    assert tq % (2 * tk) == 0 and s % tq == 0
    vec = lambda n: pl.BlockSpec((None, 1, n), lambda bi, i: (layer, 0, 0))
    lanes_per_branch = BRANCH_W // LANE
    n_chain = 2 * DA_HEADS
    return pl.pallas_call(
        functools.partial(_attn_kernel, lambda_init=lambda_init, tq=tq, tk=tk),
        grid=(b, s // tq),
        in_specs=[
            vec(DA_HEAD_DIM), vec(DA_HEAD_DIM), vec(DA_HEAD_DIM), vec(DA_HEAD_DIM), vec(DA_V_DIM),
            pl.BlockSpec((None, tq, BRANCH_W), lambda bi, i: (bi, i, COL_Q // lanes_per_branch)),
            pl.BlockSpec((None, s, BRANCH_W), lambda bi, i: (bi, 0, COL_K // lanes_per_branch)),
            pl.BlockSpec((None, s, BRANCH_W), lambda bi, i: (bi, 0, COL_V // lanes_per_branch)),
        ],
        out_specs=pl.BlockSpec((None, tq, BRANCH_W), lambda bi, i: (bi, i, 0)),
        out_shape=jax.ShapeDtypeStruct((b, s, BRANCH_W), BF16),
        scratch_shapes=[pltpu.VMEM((s // tk, DA_HEADS, DA_V_DIM + ONES_ROWS, tk), BF16),
                        pltpu.VMEM((n_chain, DA_V_DIM + ONES_ROWS, tq), F32),
                        pltpu.VMEM((n_chain, 1, tq), F32),
                        pltpu.VMEM((n_chain, tk, tq), F32),
                        pltpu.VMEM((n_chain, tk, tq), F32)],
        compiler_params=pltpu.CompilerParams(
            dimension_semantics=("arbitrary", "arbitrary"),
            vmem_limit_bytes=VMEM_LIMIT),
        name="diff_attn",
    )(lq1, lk1, lq2, lk2, sub_g, proj3d, proj3d, proj3d)


def _layer_norm(v, g, b):
    mu = jnp.mean(v, axis=-1, keepdims=True)
    d = v - mu
    var = jnp.mean(d * d, axis=-1, keepdims=True)
    return d * lax.rsqrt(var + EPS) * g + b


def _silu(v):
    return v * jax.nn.sigmoid(v)


def _merge_kernel(x_ref, aglu_ref, agate_ref, bgate_ref, u_ref, sv_ref, cgate_ref,
                  ga_ref, gb_ref, gc_ref, o_ref,
                  convw_ref, convb_ref, clng_ref, clnb_ref, slng_ref, slnb_ref, sguw_ref, sgub_ref,
                  wpa_ref, wpb_ref, wpc_ref, wo_ref, fng_ref, out_ref, zbuf, zsh, conv_scr,
                  *, t_rows, final):
    @pl.when(pl.program_id(1) == 0)
    def _():
        zbuf[0:HALO, :] = jnp.zeros((HALO, BRANCH_W), F32)

    zb = o_ref[...] * _silu(bgate_ref[...])
    yb = jnp.dot(zb, wpb_ref[...], preferred_element_type=F32)

    v = _layer_norm(sv_ref[...].astype(F32), slng_ref[...], slnb_ref[...]).astype(BF16)
    row = lax.broadcasted_iota(jnp.int32, (CHUNK, CHUNK), 0)
    col = lax.broadcasted_iota(jnp.int32, (CHUNK, CHUNK), 1)
    tril = col <= row
    ws = [jnp.where(tril, sguw_ref[gi], 0.0).astype(BF16) for gi in range(SGU_GROUPS)]
    bs = [jnp.broadcast_to(sgub_ref[:, gi:gi + 1], (CHUNK, LANE)) for gi in range(SGU_GROUPS)]
    rows = []
    for n in range(t_rows // CHUNK):
        blocks = []
        for gi in range(SGU_GROUPS):
            vb = v[n * CHUNK:(n + 1) * CHUNK, gi * LANE:(gi + 1) * LANE]
            blocks.append(jnp.dot(ws[gi], vb, preferred_element_type=F32) + bs[gi])
        rows.append(jnp.concatenate(blocks, axis=1))
    mixed = jnp.concatenate(rows, axis=0)
    zc = mixed.astype(BF16) * (u_ref[...] * _silu(cgate_ref[...]))
    yc = jnp.dot(zc, wpc_ref[...], preferred_element_type=F32)

    a = aglu_ref[:, :BRANCH_W].astype(F32)
    g = aglu_ref[:, BRANCH_W:].astype(F32)
    zbuf[HALO:HALO + t_rows, :] = a * jax.nn.sigmoid(g)
    rc = 64
    n_sh = t_rows + HALO - SUBLANE
    for c in range(BRANCH_W // LANE):
        cs = slice(c * LANE, (c + 1) * LANE)
        for sh in range(1, SUBLANE):
            zsh[sh - 1] = zbuf[sh:sh + n_sh, cs]
        for r in range(t_rows // rc):
            acc = jnp.broadcast_to(convb_ref[:, cs], (rc, LANE))
            for j in range(CONV_K):
                off = HALO - (CONV_K - 1) + j
                sh = off % SUBLANE
                base = r * rc + off - sh
                if sh == 0:
                    src = zbuf[base:base + rc, cs]
                else:
                    src = zsh[sh - 1, base:base + rc, :]
                acc = acc + convw_ref[j:j + 1, cs] * src
            conv_scr[r * rc:(r + 1) * rc, cs] = acc
    zbuf[0:HALO, :] = zbuf[t_rows:t_rows + HALO, :]
    za = _silu(_layer_norm(conv_scr[...], clng_ref[...], clnb_ref[...]))
    za = za.astype(BF16) * _silu(agate_ref[...])
    ya = jnp.dot(za, wpa_ref[...], preferred_element_type=F32)

    ga = jax.nn.sigmoid(ga_ref[...].astype(F32))
    gb = jax.nn.sigmoid(gb_ref[...].astype(F32))
    gc = jax.nn.sigmoid(gc_ref[...].astype(F32))
    merged = ga * ya + gb * yb + gc * yc
    xn = x_ref[...] + jnp.dot(merged.astype(BF16), wo_ref[...], preferred_element_type=F32)
    if final:
        ms = jnp.mean(xn * xn, axis=-1, keepdims=True)
        xn = xn * lax.rsqrt(ms + EPS) * fng_ref[...]
    out_ref[...] = xn


def _merge(x3d, proj3d, o3d, conv_w, conv_b, cln_g, cln_b, sln_g, sln_b, sgu_w, sgu_bt,
           w_pa, w_pb, w_pc, w_o, fn_g, layer, final, t_rows=512):
    b, s, _ = x3d.shape

    def pcols(col, width):
        return pl.BlockSpec((None, t_rows, width), lambda bi, t: (bi, t, (col * LANE) // width))

    def lvec(n):
        return pl.BlockSpec((None, 1, n), lambda bi, t: (layer, 0, 0))

    def lmat(r, c):
        return pl.BlockSpec((None, r, c), lambda bi, t: (layer, 0, 0))

    return pl.pallas_call(
        functools.partial(_merge_kernel, t_rows=t_rows, final=final),
        grid=(b, s // t_rows),
        in_specs=[
            pl.BlockSpec((None, t_rows, D_MODEL), lambda bi, t: (bi, t, 0)),
            pcols(COL_AGLU, 2 * BRANCH_W),
            pcols(COL_AGATE, BRANCH_W),
            pcols(COL_BGATE, BRANCH_W),
            pcols(COL_U, BRANCH_W),
            pcols(COL_SV, BRANCH_W),
            pcols(COL_CGATE, BRANCH_W),
            pcols(COL_GATES, D_MODEL),
            pcols(COL_GATES + D_MODEL // LANE, D_MODEL),
            pcols(COL_GATES + 2 * D_MODEL // LANE, D_MODEL),
            pl.BlockSpec((None, t_rows, BRANCH_W), lambda bi, t: (bi, t, 0)),
            lmat(CONV_K, BRANCH_W),
            lvec(BRANCH_W), lvec(BRANCH_W), lvec(BRANCH_W), lvec(BRANCH_W), lvec(BRANCH_W),
            pl.BlockSpec((None, SGU_GROUPS, CHUNK, CHUNK), lambda bi, t: (layer, 0, 0, 0)),
            lmat(CHUNK, SGU_GROUPS),
            lmat(BRANCH_W, D_MODEL), lmat(BRANCH_W, D_MODEL), lmat(BRANCH_W, D_MODEL),
            lmat(D_MODEL, D_MODEL),
            pl.BlockSpec((1, D_MODEL), lambda bi, t: (0, 0)),
        ],
        out_specs=pl.BlockSpec((None, t_rows, D_MODEL), lambda bi, t: (bi, t, 0)),
        out_shape=jax.ShapeDtypeStruct((b, s, D_MODEL), F32),
        scratch_shapes=[pltpu.VMEM((t_rows + HALO, BRANCH_W), F32),
                        pltpu.VMEM((SUBLANE - 1, t_rows + HALO - SUBLANE, LANE), F32),
                        pltpu.VMEM((t_rows, BRANCH_W), F32)],
        compiler_params=pltpu.CompilerParams(
            dimension_semantics=("arbitrary", "arbitrary"), vmem_limit_bytes=VMEM_LIMIT),
        name="merge",
    )(x3d, proj3d, proj3d, proj3d, proj3d, proj3d, proj3d, proj3d, proj3d, proj3d, o3d,
      conv_w, conv_b, cln_g, cln_b, sln_g, sln_b, sgu_w, sgu_bt, w_pa, w_pb, w_pc, w_o, fn_g)


def kernel(x, norm_g, w_in, conv_w, conv_b, conv_ln_g, conv_ln_b, lam_q1, lam_k1, lam_q2, lam_k2,
           diff_norm_g, sgu_ln_g, sgu_ln_b, sgu_w, sgu_b, w_pa, w_pb, w_pc, w_o, final_norm_g):
    b, s, d = x.shape
    depth = w_in.shape[0]
    row3 = lambda p: p[:, None, :]
    w_in_b, w_pa_b, w_pb_b, w_pc_b, w_o_b = (w.astype(BF16) for w in (w_in, w_pa, w_pb, w_pc, w_o))
    norm_g3, conv_b3, cln_g3, cln_b3 = row3(norm_g), row3(conv_b), row3(conv_ln_g), row3(conv_ln_b)
    sln_g3, sln_b3, sub_g3 = row3(sgu_ln_g), row3(sgu_ln_b), row3(diff_norm_g)
    lq1, lk1, lq2, lk2 = row3(lam_q1), row3(lam_k1), row3(lam_q2), row3(lam_k2)
    sgu_bt = jnp.swapaxes(sgu_b, 1, 2)
    fn_g = final_norm_g[None, :]
    col_scale = jnp.ones((1, IN_COLS), F32).at[:, COL_Q * LANE:COL_K * LANE].set(Q_SCALE)

    for layer in range(depth):
        lambda_init = 0.8 - 0.6 * math.exp(-0.3 * layer)
        proj = _inproj(x.reshape(b * s, d), norm_g3, w_in_b, col_scale, layer).reshape(b, s, IN_COLS)
        o = _attention(proj, lq1, lk1, lq2, lk2, sub_g3, layer, lambda_init)
        x = _merge(x, proj, o, conv_w, conv_b3, cln_g3, cln_b3, sln_g3, sln_b3, sgu_w, sgu_bt,
                   w_pa_b, w_pb_b, w_pc_b, w_o_b, fn_g, layer, final=(layer == depth - 1))
    return x
```

```python
import functools
import math

import jax
import jax.numpy as jnp
from jax import lax
from jax.experimental import pallas as pl
from jax.experimental.pallas import tpu as pltpu

D_MODEL = 1024
BRANCH_W = D_MODEL // 2
CONV_K = 31
DA_HEADS = 4
DA_HEAD_DIM = 64
DA_V_DIM = 128
SGU_GROUPS = 4
CHUNK = 128
IN_COLS = 8192
EPS = 1e-6

LANE = 128
SUBLANE = 8
COL_AGLU = 0
COL_AGATE = 1024 // LANE
COL_Q = 1536 // LANE
COL_K = 2048 // LANE
COL_V = 2560 // LANE
COL_BGATE = 3072 // LANE
COL_U = 3584 // LANE
COL_SV = 4096 // LANE
COL_CGATE = 4608 // LANE
COL_GATES = 5120 // LANE

ONES_ROWS = 16
Q_SCALE = DA_HEAD_DIM ** -0.5 * math.log2(math.e)
HALO = 32
NEG = -1e30

F32 = jnp.float32
BF16 = jnp.bfloat16

VMEM_LIMIT = 56 * 1024 * 1024


def _inproj_kernel(x_ref, g_ref, w_ref, cs_ref, o_ref, h_scr):
    @pl.when(pl.program_id(1) == 0)
    def _():
        x = x_ref[...]
        ms = jnp.mean(x * x, axis=-1, keepdims=True)
        h_scr[...] = (x * lax.rsqrt(ms + EPS) * g_ref[...]).astype(BF16)

    acc = jnp.dot(h_scr[...], w_ref[...], preferred_element_type=F32)
    o_ref[...] = (acc * cs_ref[...]).astype(BF16)


def _inproj(x2d, g, w_bf16, col_scale, layer, tm=2048, tn=2048):
    m = x2d.shape[0]
    return pl.pallas_call(
        _inproj_kernel,
        grid=(m // tm, IN_COLS // tn),
        in_specs=[
            pl.BlockSpec((tm, D_MODEL), lambda i, j: (i, 0)),
            pl.BlockSpec((None, 1, D_MODEL), lambda i, j: (layer, 0, 0)),
            pl.BlockSpec((None, D_MODEL, tn), lambda i, j: (layer, 0, j)),
            pl.BlockSpec((1, tn), lambda i, j: (0, j)),
        ],
        out_specs=pl.BlockSpec((tm, tn), lambda i, j: (i, j)),
        out_shape=jax.ShapeDtypeStruct((m, IN_COLS), BF16),
        scratch_shapes=[pltpu.VMEM((tm, D_MODEL), BF16)],
        compiler_params=pltpu.CompilerParams(
            dimension_semantics=("arbitrary", "arbitrary"), vmem_limit_bytes=VMEM_LIMIT),
        name="inproj",
    )(x2d, g, w_bf16, col_scale)


def _attn_kernel(lq1_ref, lk1_ref, lq2_ref, lk2_ref, sg_ref, q_ref, k_ref, v_ref, o_ref,
                 vt_scr, acc_scr, m_scr, sa_scr, sb_scr, *, lambda_init, tq, tk):
    i = pl.program_id(1)
    n_kv = v_ref.shape[0] // tk
    n_chain = 2 * DA_HEADS
    n_diag = tq // tk

    @pl.when(i == 0)
    def _():
        ones = jnp.ones((ONES_ROWS, tk), BF16)

        def fill(c, carry):
            vblk = v_ref[pl.ds(pl.multiple_of(c * tk, tk), tk), :]
            vt = vblk.astype(F32).T.astype(BF16)
            for h in range(DA_HEADS):
                vt_scr[c, h, 0:DA_V_DIM, :] = vt[h * DA_V_DIM:(h + 1) * DA_V_DIM, :]
                vt_scr[c, h, DA_V_DIM:, :] = ones
            return carry
        lax.fori_loop(0, n_kv, fill, 0)

    lam = (jnp.exp(jnp.sum(lq1_ref[...] * lk1_ref[...], axis=-1, keepdims=True))
           - jnp.exp(jnp.sum(lq2_ref[...] * lk2_ref[...], axis=-1, keepdims=True))
           + lambda_init)

    qt = q_ref[...].astype(F32).T
    sub = lax.broadcasted_iota(jnp.int32, (LANE, tq), 0)
    qtz = []
    for h in range(DA_HEADS):
        qth = qt[h * LANE:(h + 1) * LANE, :]
        qtz.append(jnp.where(sub < DA_HEAD_DIM, qth, 0.0).astype(BF16))
        qtz.append(jnp.where(sub >= DA_HEAD_DIM, qth, 0.0).astype(BF16))

    rel = (lax.broadcasted_iota(jnp.int32, (tk, tq), 0)
           - lax.broadcasted_iota(jnp.int32, (tk, tq), 1))

    acc_scr[...] = jnp.zeros(acc_scr.shape, F32)
    m_scr[...] = jnp.full(m_scr.shape, NEG, F32)

    def scores_into(j, dst, q0=0):
        kb = k_ref[pl.ds(pl.multiple_of(j * tk, tk), tk), :]
        for c in range(n_chain):
            dst[c, :, q0:] = jnp.dot(kb[:, (c // 2) * LANE:(c // 2 + 1) * LANE], qtz[c][:, q0:],
                                     preferred_element_type=F32)

    def softmax_pv(j, src, diag):
        q0 = 0 if diag is None else diag * tk
        pts, alphas = [], []
        for c in range(n_chain):
            st = src[c, :, q0:]
            if diag is not None:
                st = jnp.where(rel[:, q0:] <= -diag * tk, st, NEG)
            m = m_scr[c, :, q0:]
            m_new = jnp.maximum(m, jnp.max(st, axis=0, keepdims=True))
            m_scr[c, :, q0:] = m_new
            alphas.append(jnp.exp2(m - m_new))
            pts.append(jnp.exp2(st - m_new).astype(BF16))
        for c in range(n_chain):
            acc_scr[c, :, q0:] = alphas[c] * acc_scr[c, :, q0:] + jnp.dot(
                vt_scr[j, c // 2], pts[c], preferred_element_type=F32)

    bufs = (sa_scr, sb_scr)
    scores_into(0, sa_scr)

    def pair(j):
        scores_into(j + 1, sb_scr)
        softmax_pv(j, sa_scr, None)
        scores_into(j + 2, sa_scr)
        softmax_pv(j + 1, sb_scr, None)

    def quad(t, carry):
        pair(4 * t)
        pair(4 * t + 2)
        return carry
    n_quad = lax.div(i, 2)
    lax.fori_loop(0, n_quad, quad, 0)

    @pl.when(lax.rem(i, 2) == 1)
    def _():
        pair(4 * n_quad)
    first = n_diag * i
    for d in range(n_diag):
        if d + 1 < n_diag:
            scores_into(first + d + 1, bufs[(d + 1) % 2], (d + 1) * tk)
        softmax_pv(first + d, bufs[d % 2], d)

    def normalised(c):
        return acc_scr[c, 0:DA_V_DIM, :] * (1.0 / acc_scr[c, DA_V_DIM:DA_V_DIM + 1, :])

    for h in range(DA_HEADS):
        ot = normalised(2 * h) - lam * normalised(2 * h + 1)
        o = ot.T
        ms = jnp.mean(o * o, axis=-1, keepdims=True)
        o = o * lax.rsqrt(ms + EPS) * sg_ref[...] * (1.0 - lambda_init)
        o_ref[:, h * DA_V_DIM:(h + 1) * DA_V_DIM] = o.astype(o_ref.dtype)


def _attention(proj3d, lq1, lk1, lq2, lk2, sub_g, layer, lambda_init, tq=512, tk=256):
    b, s, _ = proj3d.shape
    assert tq == 2 * tk and s % tq == 0
    vec = lambda n: pl.BlockSpec((None, 1, n), lambda bi, i: (layer, 0, 0))
    lanes_per_branch = BRANCH_W // LANE
    n_chain = 2 * DA_HEADS
    return pl.pallas_call(
        functools.partial(_attn_kernel, lambda_init=lambda_init, tq=tq, tk=tk),
        grid=(b, s // tq),
        in_specs=[
            vec(DA_HEAD_DIM), vec(DA_HEAD_DIM), vec(DA_HEAD_DIM), vec(DA_HEAD_DIM), vec(DA_V_DIM),
            pl.BlockSpec((None, tq, BRANCH_W), lambda bi, i: (bi, i, COL_Q // lanes_per_branch)),
            pl.BlockSpec((None, s, BRANCH_W), lambda bi, i: (bi, 0, COL_K // lanes_per_branch)),
            pl.BlockSpec((None, s, BRANCH_W), lambda bi, i: (bi, 0, COL_V // lanes_per_branch)),
        ],
        out_specs=pl.BlockSpec((None, tq, BRANCH_W), lambda bi, i: (bi, i, 0)),
        out_shape=jax.ShapeDtypeStruct((b, s, BRANCH_W), BF16),
        scratch_shapes=[pltpu.VMEM((s // tk, DA_HEADS, DA_V_DIM + ONES_ROWS, tk), BF16),
                        pltpu.VMEM((n_chain, DA_V_DIM + ONES_ROWS, tq), F32),
                        pltpu.VMEM((n_chain, 1, tq), F32),
                        pltpu.VMEM((n_chain, tk, tq), F32),
                        pltpu.VMEM((n_chain, tk, tq), F32)],
        compiler_params=pltpu.CompilerParams(
            dimension_semantics=("arbitrary", "arbitrary"),
            vmem_limit_bytes=VMEM_LIMIT),
        name="diff_attn",
    )(lq1, lk1, lq2, lk2, sub_g, proj3d, proj3d, proj3d)


def _layer_norm(v, g, b):
    mu = jnp.mean(v, axis=-1, keepdims=True)
    d = v - mu
    var = jnp.mean(d * d, axis=-1, keepdims=True)
    return d * lax.rsqrt(var + EPS) * g + b


def _silu(v):
    return v * jax.nn.sigmoid(v)


def _merge_kernel(x_ref, aglu_ref, agate_ref, bgate_ref, u_ref, sv_ref, cgate_ref,
                  ga_ref, gb_ref, gc_ref, o_ref,
                  convw_ref, convb_ref, clng_ref, clnb_ref, slng_ref, slnb_ref, sguw_ref, sgub_ref,
                  wpa_ref, wpb_ref, wpc_ref, wo_ref, fng_ref, out_ref, zbuf, zsh, conv_scr,
                  *, t_rows, final):
    @pl.when(pl.program_id(1) == 0)
    def _():
        zbuf[0:HALO, :] = jnp.zeros((HALO, BRANCH_W), F32)

    zb = o_ref[...].astype(F32) * _silu(bgate_ref[...].astype(F32))
    yb = jnp.dot(zb.astype(BF16), wpb_ref[...], preferred_element_type=F32)

    v = _layer_norm(sv_ref[...].astype(F32), slng_ref[...], slnb_ref[...]).astype(BF16)
    row = lax.broadcasted_iota(jnp.int32, (CHUNK, CHUNK), 0)
    col = lax.broadcasted_iota(jnp.int32, (CHUNK, CHUNK), 1)
    tril = col <= row
    ws = [jnp.where(tril, sguw_ref[gi], 0.0).astype(BF16) for gi in range(SGU_GROUPS)]
    bs = [jnp.broadcast_to(sgub_ref[:, gi:gi + 1], (CHUNK, LANE)) for gi in range(SGU_GROUPS)]
    rows = []
    for n in range(t_rows // CHUNK):
        blocks = []
        for gi in range(SGU_GROUPS):
            vb = v[n * CHUNK:(n + 1) * CHUNK, gi * LANE:(gi + 1) * LANE]
            blocks.append(jnp.dot(ws[gi], vb, preferred_element_type=F32) + bs[gi])
        rows.append(jnp.concatenate(blocks, axis=1))
    mixed = jnp.concatenate(rows, axis=0)
    zc = u_ref[...].astype(F32) * mixed * _silu(cgate_ref[...].astype(F32))
    yc = jnp.dot(zc.astype(BF16), wpc_ref[...], preferred_element_type=F32)

    a = aglu_ref[:, :BRANCH_W].astype(F32)
    g = aglu_ref[:, BRANCH_W:].astype(F32)
    zbuf[HALO:HALO + t_rows, :] = a * jax.nn.sigmoid(g)
    rc = 64
    n_sh = t_rows + HALO - SUBLANE
    for c in range(BRANCH_W // LANE):
        cs = slice(c * LANE, (c + 1) * LANE)
        for sh in range(1, SUBLANE):
            zsh[sh - 1] = zbuf[sh:sh + n_sh, cs]
        for r in range(t_rows // rc):
            acc = jnp.broadcast_to(convb_ref[:, cs], (rc, LANE))
            for j in range(CONV_K):
                off = HALO - (CONV_K - 1) + j
                sh = off % SUBLANE
                base = r * rc + off - sh
                if sh == 0:
                    src = zbuf[base:base + rc, cs]
                else:
                    src = zsh[sh - 1, base:base + rc, :]
                acc = acc + convw_ref[j:j + 1, cs] * src
            conv_scr[r * rc:(r + 1) * rc, cs] = acc
    zbuf[0:HALO, :] = zbuf[t_rows:t_rows + HALO, :]
    za = _silu(_layer_norm(conv_scr[...], clng_ref[...], clnb_ref[...]))
    za = za * _silu(agate_ref[...].astype(F32))
    ya = jnp.dot(za.astype(BF16), wpa_ref[...], preferred_element_type=F32)

    ga = jax.nn.sigmoid(ga_ref[...].astype(F32))
    gb = jax.nn.sigmoid(gb_ref[...].astype(F32))
    gc = jax.nn.sigmoid(gc_ref[...].astype(F32))
    merged = ga * ya + gb * yb + gc * yc
    xn = x_ref[...] + jnp.dot(merged.astype(BF16), wo_ref[...], preferred_element_type=F32)
    if final:
        ms = jnp.mean(xn * xn, axis=-1, keepdims=True)
        xn = xn * lax.rsqrt(ms + EPS) * fng_ref[...]
    out_ref[...] = xn


def _merge(x3d, proj3d, o3d, conv_w, conv_b, cln_g, cln_b, sln_g, sln_b, sgu_w, sgu_bt,
           w_pa, w_pb, w_pc, w_o, fn_g, layer, final, t_rows=512):
    b, s, _ = x3d.shape

    def pcols(col, width):
        return pl.BlockSpec((None, t_rows, width), lambda bi, t: (bi, t, (col * LANE) // width))

    def lvec(n):
        return pl.BlockSpec((None, 1, n), lambda bi, t: (layer, 0, 0))

    def lmat(r, c):
        return pl.BlockSpec((None, r, c), lambda bi, t: (layer, 0, 0))

    return pl.pallas_call(
        functools.partial(_merge_kernel, t_rows=t_rows, final=final),
        grid=(b, s // t_rows),
        in_specs=[
            pl.BlockSpec((None, t_rows, D_MODEL), lambda bi, t: (bi, t, 0)),
            pcols(COL_AGLU, 2 * BRANCH_W),
            pcols(COL_AGATE, BRANCH_W),
            pcols(COL_BGATE, BRANCH_W),
            pcols(COL_U, BRANCH_W),
            pcols(COL_SV, BRANCH_W),
            pcols(COL_CGATE, BRANCH_W),
            pcols(COL_GATES, D_MODEL),
            pcols(COL_GATES + D_MODEL // LANE, D_MODEL),
            pcols(COL_GATES + 2 * D_MODEL // LANE, D_MODEL),
            pl.BlockSpec((None, t_rows, BRANCH_W), lambda bi, t: (bi, t, 0)),
            lmat(CONV_K, BRANCH_W),
            lvec(BRANCH_W), lvec(BRANCH_W), lvec(BRANCH_W), lvec(BRANCH_W), lvec(BRANCH_W),
            pl.BlockSpec((None, SGU_GROUPS, CHUNK, CHUNK), lambda bi, t: (layer, 0, 0, 0)),
            lmat(CHUNK, SGU_GROUPS),
            lmat(BRANCH_W, D_MODEL), lmat(BRANCH_W, D_MODEL), lmat(BRANCH_W, D_MODEL),
            lmat(D_MODEL, D_MODEL),
            pl.BlockSpec((1, D_MODEL), lambda bi, t: (0, 0)),
        ],
        out_specs=pl.BlockSpec((None, t_rows, D_MODEL), lambda bi, t: (bi, t, 0)),
        out_shape=jax.ShapeDtypeStruct((b, s, D_MODEL), F32),
        scratch_shapes=[pltpu.VMEM((t_rows + HALO, BRANCH_W), F32),
                        pltpu.VMEM((SUBLANE - 1, t_rows + HALO - SUBLANE, LANE), F32),
                        pltpu.VMEM((t_rows, BRANCH_W), F32)],
        compiler_params=pltpu.CompilerParams(
            dimension_semantics=("arbitrary", "arbitrary"), vmem_limit_bytes=VMEM_LIMIT),
        name="merge",
    )(x3d, proj3d, proj3d, proj3d, proj3d, proj3d, proj3d, proj3d, proj3d, proj3d, o3d,
      conv_w, conv_b, cln_g, cln_b, sln_g, sln_b, sgu_w, sgu_bt, w_pa, w_pb, w_pc, w_o, fn_g)


def kernel(x, norm_g, w_in, conv_w, conv_b, conv_ln_g, conv_ln_b, lam_q1, lam_k1, lam_q2, lam_k2,
           diff_norm_g, sgu_ln_g, sgu_ln_b, sgu_w, sgu_b, w_pa, w_pb, w_pc, w_o, final_norm_g):
    b, s, d = x.shape
    depth = w_in.shape[0]
    row3 = lambda p: p[:, None, :]
    w_in_b, w_pa_b, w_pb_b, w_pc_b, w_o_b = (w.astype(BF16) for w in (w_in, w_pa, w_pb, w_pc, w_o))
    norm_g3, conv_b3, cln_g3, cln_b3 = row3(norm_g), row3(conv_b), row3(conv_ln_g), row3(conv_ln_b)
    sln_g3, sln_b3, sub_g3 = row3(sgu_ln_g), row3(sgu_ln_b), row3(diff_norm_g)
    lq1, lk1, lq2, lk2 = row3(lam_q1), row3(lam_k1), row3(lam_q2), row3(lam_k2)
    sgu_bt = jnp.swapaxes(sgu_b, 1, 2)
    fn_g = final_norm_g[None, :]
    col_scale = jnp.ones((1, IN_COLS), F32).at[:, COL_Q * LANE:COL_K * LANE].set(Q_SCALE)

    for layer in range(depth):
        lambda_init = 0.8 - 0.6 * math.exp(-0.3 * layer)
        proj = _inproj(x.reshape(b * s, d), norm_g3, w_in_b, col_scale, layer).reshape(b, s, IN_COLS)
        o = _attention(proj, lq1, lk1, lq2, lk2, sub_g3, layer, lambda_init)
        x = _merge(x, proj, o, conv_w, conv_b3, cln_g3, cln_b3, sln_g3, sln_b3, sgu_w, sgu_bt,
                   w_pa_b, w_pb_b, w_pc_b, w_o_b, fn_g, layer, final=(layer == depth - 1))
    return x
```

```python
import functools
import math

import jax
import jax.numpy as jnp
from jax import lax
from jax.experimental import pallas as pl
from jax.experimental.pallas import tpu as pltpu

D_MODEL = 1024
BRANCH_W = D_MODEL // 2
CONV_K = 31
DA_HEADS = 4
DA_HEAD_DIM = 64
DA_V_DIM = 128
SGU_GROUPS = 4
CHUNK = 128
IN_COLS = 8192
EPS = 1e-6

LANE = 128
SUBLANE = 8
COL_AGLU = 0
COL_AGATE = 1024 // LANE
COL_Q = 1536 // LANE
COL_K = 2048 // LANE
COL_V = 2560 // LANE
COL_BGATE = 3072 // LANE
COL_U = 3584 // LANE
COL_SV = 4096 // LANE
COL_CGATE = 4608 // LANE
COL_GATES = 5120 // LANE

ONES_ROWS = 16
Q_SCALE = DA_HEAD_DIM ** -0.5 * math.log2(math.e)
HALO = 32
NEG = -1e30

F32 = jnp.float32
BF16 = jnp.bfloat16

VMEM_LIMIT = 56 * 1024 * 1024


def _inproj_kernel(x_ref, g_ref, w_ref, cs_ref, o_ref, h_scr):
    @pl.when(pl.program_id(1) == 0)
    def _():
        x = x_ref[...]
        ms = jnp.mean(x * x, axis=-1, keepdims=True)
        h_scr[...] = (x * lax.rsqrt(ms + EPS) * g_ref[...]).astype(BF16)

    acc = jnp.dot(h_scr[...], w_ref[...], preferred_element_type=F32)
    o_ref[...] = (acc * cs_ref[...]).astype(BF16)


def _inproj(x2d, g, w_bf16, col_scale, layer, tm=2048, tn=2048):
    m = x2d.shape[0]
    return pl.pallas_call(
        _inproj_kernel,
        grid=(m // tm, IN_COLS // tn),
        in_specs=[
            pl.BlockSpec((tm, D_MODEL), lambda i, j: (i, 0)),
            pl.BlockSpec((None, 1, D_MODEL), lambda i, j: (layer, 0, 0)),
            pl.BlockSpec((None, D_MODEL, tn), lambda i, j: (layer, 0, j)),
            pl.BlockSpec((1, tn), lambda i, j: (0, j)),
        ],
        out_specs=pl.BlockSpec((tm, tn), lambda i, j: (i, j)),
        out_shape=jax.ShapeDtypeStruct((m, IN_COLS), BF16),
        scratch_shapes=[pltpu.VMEM((tm, D_MODEL), BF16)],
        compiler_params=pltpu.CompilerParams(
            dimension_semantics=("arbitrary", "arbitrary"), vmem_limit_bytes=VMEM_LIMIT),
        name="inproj",
    )(x2d, g, w_bf16, col_scale)


def _attn_kernel(lq1_ref, lk1_ref, lq2_ref, lk2_ref, sg_ref, q_ref, k_ref, v_ref, o_ref,
                 vt_scr, acc_scr, m_scr, sa_scr, sb_scr, *, lambda_init, tq, tk):
    i = pl.program_id(1)
    n_kv = v_ref.shape[0] // tk
    n_chain = 2 * DA_HEADS
    n_diag = tq // tk

    @pl.when(i == 0)
    def _():
        ones = jnp.ones((ONES_ROWS, tk), BF16)

        def fill(c, carry):
            vblk = v_ref[pl.ds(pl.multiple_of(c * tk, tk), tk), :]
            vt = vblk.astype(F32).T.astype(BF16)
            for h in range(DA_HEADS):
                vt_scr[c, h, 0:DA_V_DIM, :] = vt[h * DA_V_DIM:(h + 1) * DA_V_DIM, :]
                vt_scr[c, h, DA_V_DIM:, :] = ones
            return carry
        lax.fori_loop(0, n_kv, fill, 0)

    lam = (jnp.exp(jnp.sum(lq1_ref[...] * lk1_ref[...], axis=-1, keepdims=True))
           - jnp.exp(jnp.sum(lq2_ref[...] * lk2_ref[...], axis=-1, keepdims=True))
           + lambda_init)

    qt = q_ref[...].astype(F32).T
    sub = lax.broadcasted_iota(jnp.int32, (LANE, tq), 0)
    qtz = []
    for h in range(DA_HEADS):
        qth = qt[h * LANE:(h + 1) * LANE, :]
        qtz.append(jnp.where(sub < DA_HEAD_DIM, qth, 0.0).astype(BF16))
        qtz.append(jnp.where(sub >= DA_HEAD_DIM, qth, 0.0).astype(BF16))

    rel = (lax.broadcasted_iota(jnp.int32, (tk, tq), 0)
           - lax.broadcasted_iota(jnp.int32, (tk, tq), 1))

    acc_scr[...] = jnp.zeros(acc_scr.shape, F32)
    m_scr[...] = jnp.full(m_scr.shape, NEG, F32)

    def scores_into(j, dst, q0=0):
        kb = k_ref[pl.ds(pl.multiple_of(j * tk, tk), tk), :]
        for c in range(n_chain):
            dst[c, :, q0:] = jnp.dot(kb[:, (c // 2) * LANE:(c // 2 + 1) * LANE], qtz[c][:, q0:],
                                     preferred_element_type=F32)

    def softmax_pv(j, src, diag):
        q0 = 0 if diag is None else diag * tk
        pts, alphas = [], []
        for c in range(n_chain):
            st = src[c, :, q0:]
            if diag is not None:
                st = jnp.where(rel[:, q0:] <= -diag * tk, st, NEG)
            m = m_scr[c, :, q0:]
            m_new = jnp.maximum(m, jnp.max(st, axis=0, keepdims=True))
            m_scr[c, :, q0:] = m_new
            alphas.append(jnp.exp2(m - m_new))
            pts.append(jnp.exp2(st - m_new).astype(BF16))
        for c in range(n_chain):
            acc_scr[c, :, q0:] = alphas[c] * acc_scr[c, :, q0:] + jnp.dot(
                vt_scr[j, c // 2], pts[c], preferred_element_type=F32)

    bufs = (sa_scr, sb_scr)
    scores_into(0, sa_scr)

    def pair(j):
        scores_into(j + 1, sb_scr)
        softmax_pv(j, sa_scr, None)
        scores_into(j + 2, sa_scr)
        softmax_pv(j + 1, sb_scr, None)

    def quad(j):
        pair(j)
        pair(j + 2)

    def octet(t, carry):
        quad(8 * t)
        quad(8 * t + 4)
        return carry
    n_oct = lax.div(i, 4)
    lax.fori_loop(0, n_oct, octet, 0)
    left = lax.rem(i, 4)

    @pl.when(left >= 2)
    def _():
        quad(8 * n_oct)

    @pl.when(lax.rem(left, 2) == 1)
    def _():
        pair(8 * n_oct + 4 * lax.div(left, 2))
    first = n_diag * i
    for d in range(n_diag):
        if d + 1 < n_diag:
            scores_into(first + d + 1, bufs[(d + 1) % 2], (d + 1) * tk)
        softmax_pv(first + d, bufs[d % 2], d)

    def normalised(c):
        return acc_scr[c, 0:DA_V_DIM, :] * (1.0 / acc_scr[c, DA_V_DIM:DA_V_DIM + 1, :])

    for h in range(DA_HEADS):
        ot = normalised(2 * h) - lam * normalised(2 * h + 1)
        o = ot.T
        ms = jnp.mean(o * o, axis=-1, keepdims=True)
        o = o * lax.rsqrt(ms + EPS) * sg_ref[...] * (1.0 - lambda_init)
        o_ref[:, h * DA_V_DIM:(h + 1) * DA_V_DIM] = o.astype(o_ref.dtype)


def _attention(proj3d, lq1, lk1, lq2, lk2, sub_g, layer, lambda_init, tq=512, tk=256):
    b, s, _ = proj3d.shape
    assert tq == 2 * tk and s % tq == 0
    vec = lambda n: pl.BlockSpec((None, 1, n), lambda bi, i: (layer, 0, 0))
    lanes_per_branch = BRANCH_W // LANE
    n_chain = 2 * DA_HEADS
    return pl.pallas_call(
        functools.partial(_attn_kernel, lambda_init=lambda_init, tq=tq, tk=tk),
        grid=(b, s // tq),
        in_specs=[
            vec(DA_HEAD_DIM), vec(DA_HEAD_DIM), vec(DA_HEAD_DIM), vec(DA_HEAD_DIM), vec(DA_V_DIM),
            pl.BlockSpec((None, tq, BRANCH_W), lambda bi, i: (bi, i, COL_Q // lanes_per_branch)),
            pl.BlockSpec((None, s, BRANCH_W), lambda bi, i: (bi, 0, COL_K // lanes_per_branch)),
            pl.BlockSpec((None, s, BRANCH_W), lambda bi, i: (bi, 0, COL_V // lanes_per_branch)),
        ],
        out_specs=pl.BlockSpec((None, tq, BRANCH_W), lambda bi, i: (bi, i, 0)),
        out_shape=jax.ShapeDtypeStruct((b, s, BRANCH_W), BF16),
        scratch_shapes=[pltpu.VMEM((s // tk, DA_HEADS, DA_V_DIM + ONES_ROWS, tk), BF16),
                        pltpu.VMEM((n_chain, DA_V_DIM + ONES_ROWS, tq), F32),
                        pltpu.VMEM((n_chain, 1, tq), F32),
                        pltpu.VMEM((n_chain, tk, tq), F32),
                        pltpu.VMEM((n_chain, tk, tq), F32)],
        compiler_params=pltpu.CompilerParams(
            dimension_semantics=("arbitrary", "arbitrary"),
            vmem_limit_bytes=VMEM_LIMIT),
        name="diff_attn",
    )(lq1, lk1, lq2, lk2, sub_g, proj3d, proj3d, proj3d)


def _layer_norm(v, g, b):
    mu = jnp.mean(v, axis=-1, keepdims=True)
    d = v - mu
    var = jnp.mean(d * d, axis=-1, keepdims=True)
    return d * lax.rsqrt(var + EPS) * g + b


def _silu(v):
    return v * jax.nn.sigmoid(v)


def _merge_kernel(x_ref, aglu_ref, agate_ref, bgate_ref, u_ref, sv_ref, cgate_ref,
                  ga_ref, gb_ref, gc_ref, o_ref,
                  convw_ref, convb_ref, clng_ref, clnb_ref, slng_ref, slnb_ref, sguw_ref, sgub_ref,
                  wpa_ref, wpb_ref, wpc_ref, wo_ref, fng_ref, out_ref, zbuf, zsh, conv_scr,
                  *, t_rows, final):
    @pl.when(pl.program_id(1) == 0)
    def _():
        zbuf[0:HALO, :] = jnp.zeros((HALO, BRANCH_W), F32)

    zb = o_ref[...].astype(F32) * _silu(bgate_ref[...].astype(F32))
    yb = jnp.dot(zb.astype(BF16), wpb_ref[...], preferred_element_type=F32)

    v = _layer_norm(sv_ref[...].astype(F32), slng_ref[...], slnb_ref[...]).astype(BF16)
    row = lax.broadcasted_iota(jnp.int32, (CHUNK, CHUNK), 0)
    col = lax.broadcasted_iota(jnp.int32, (CHUNK, CHUNK), 1)
    tril = col <= row
    ws = [jnp.where(tril, sguw_ref[gi], 0.0).astype(BF16) for gi in range(SGU_GROUPS)]
    bs = [jnp.broadcast_to(sgub_ref[:, gi:gi + 1], (CHUNK, LANE)) for gi in range(SGU_GROUPS)]
    rows = []
    for n in range(t_rows // CHUNK):
        blocks = []
        for gi in range(SGU_GROUPS):
            vb = v[n * CHUNK:(n + 1) * CHUNK, gi * LANE:(gi + 1) * LANE]
            blocks.append(jnp.dot(ws[gi], vb, preferred_element_type=F32) + bs[gi])
        rows.append(jnp.concatenate(blocks, axis=1))
    mixed = jnp.concatenate(rows, axis=0)
    zc = u_ref[...].astype(F32) * mixed * _silu(cgate_ref[...].astype(F32))
    yc = jnp.dot(zc.astype(BF16), wpc_ref[...], preferred_element_type=F32)

    a = aglu_ref[:, :BRANCH_W].astype(F32)
    g = aglu_ref[:, BRANCH_W:].astype(F32)
    zbuf[HALO:HALO + t_rows, :] = a * jax.nn.sigmoid(g)
    rc = 64
    n_sh = t_rows + HALO - SUBLANE
    for c in range(BRANCH_W // LANE):
        cs = slice(c * LANE, (c + 1) * LANE)
        for sh in range(1, SUBLANE):
            zsh[sh - 1] = zbuf[sh:sh + n_sh, cs]
        for r in range(t_rows // rc):
            acc = jnp.broadcast_to(convb_ref[:, cs], (rc, LANE))
            for j in range(CONV_K):
                off = HALO - (CONV_K - 1) + j
                sh = off % SUBLANE
                base = r * rc + off - sh
                if sh == 0:
                    src = zbuf[base:base + rc, cs]
                else:
                    src = zsh[sh - 1, base:base + rc, :]
                acc = acc + convw_ref[j:j + 1, cs] * src
            conv_scr[r * rc:(r + 1) * rc, cs] = acc
    zbuf[0:HALO, :] = zbuf[t_rows:t_rows + HALO, :]
    za = _silu(_layer_norm(conv_scr[...], clng_ref[...], clnb_ref[...]))
    za = za * _silu(agate_ref[...].astype(F32))
    ya = jnp.dot(za.astype(BF16), wpa_ref[...], preferred_element_type=F32)

    ga = jax.nn.sigmoid(ga_ref[...].astype(F32))
    gb = jax.nn.sigmoid(gb_ref[...].astype(F32))
    gc = jax.nn.sigmoid(gc_ref[...].astype(F32))
    merged = ga * ya + gb * yb + gc * yc
    xn = x_ref[...] + jnp.dot(merged.astype(BF16), wo_ref[...], preferred_element_type=F32)
    if final:
        ms = jnp.mean(xn * xn, axis=-1, keepdims=True)
        xn = xn * lax.rsqrt(ms + EPS) * fng_ref[...]
    out_ref[...] = xn


def _merge(x3d, proj3d, o3d, conv_w, conv_b, cln_g, cln_b, sln_g, sln_b, sgu_w, sgu_bt,
           w_pa, w_pb, w_pc, w_o, fn_g, layer, final, t_rows=512):
    b, s, _ = x3d.shape

    def pcols(col, width):
        return pl.BlockSpec((None, t_rows, width), lambda bi, t: (bi, t, (col * LANE) // width))

    def lvec(n):
        return pl.BlockSpec((None, 1, n), lambda bi, t: (layer, 0, 0))

    def lmat(r, c):
        return pl.BlockSpec((None, r, c), lambda bi, t: (layer, 0, 0))

    return pl.pallas_call(
        functools.partial(_merge_kernel, t_rows=t_rows, final=final),
        grid=(b, s // t_rows),
        in_specs=[
            pl.BlockSpec((None, t_rows, D_MODEL), lambda bi, t: (bi, t, 0)),
            pcols(COL_AGLU, 2 * BRANCH_W),
            pcols(COL_AGATE, BRANCH_W),
            pcols(COL_BGATE, BRANCH_W),
            pcols(COL_U, BRANCH_W),
            pcols(COL_SV, BRANCH_W),
            pcols(COL_CGATE, BRANCH_W),
            pcols(COL_GATES, D_MODEL),
            pcols(COL_GATES + D_MODEL // LANE, D_MODEL),
            pcols(COL_GATES + 2 * D_MODEL // LANE, D_MODEL),
            pl.BlockSpec((None, t_rows, BRANCH_W), lambda bi, t: (bi, t, 0)),
            lmat(CONV_K, BRANCH_W),
            lvec(BRANCH_W), lvec(BRANCH_W), lvec(BRANCH_W), lvec(BRANCH_W), lvec(BRANCH_W),
            pl.BlockSpec((None, SGU_GROUPS, CHUNK, CHUNK), lambda bi, t: (layer, 0, 0, 0)),
            lmat(CHUNK, SGU_GROUPS),
            lmat(BRANCH_W, D_MODEL), lmat(BRANCH_W, D_MODEL), lmat(BRANCH_W, D_MODEL),
            lmat(D_MODEL, D_MODEL),
            pl.BlockSpec((1, D_MODEL), lambda bi, t: (0, 0)),
        ],
        out_specs=pl.BlockSpec((None, t_rows, D_MODEL), lambda bi, t: (bi, t, 0)),
        out_shape=jax.ShapeDtypeStruct((b, s, D_MODEL), F32),
        scratch_shapes=[pltpu.VMEM((t_rows + HALO, BRANCH_W), F32),
                        pltpu.VMEM((SUBLANE - 1, t_rows + HALO - SUBLANE, LANE), F32),
                        pltpu.VMEM((t_rows, BRANCH_W), F32)],
        compiler_params=pltpu.CompilerParams(
            dimension_semantics=("arbitrary", "arbitrary"), vmem_limit_bytes=VMEM_LIMIT),
        name="merge",
    )(x3d, proj3d, proj3d, proj3d, proj3d, proj3d, proj3d, proj3d, proj3d, proj3d, o3d,
      conv_w, conv_b, cln_g, cln_b, sln_g, sln_b, sgu_w, sgu_bt, w_pa, w_pb, w_pc, w_o, fn_g)


def kernel(x, norm_g, w_in, conv_w, conv_b, conv_ln_g, conv_ln_b, lam_q1, lam_k1, lam_q2, lam_k2,
           diff_norm_g, sgu_ln_g, sgu_ln_b, sgu_w, sgu_b, w_pa, w_pb, w_pc, w_o, final_norm_g):
    b, s, d = x.shape
    depth = w_in.shape[0]
    row3 = lambda p: p[:, None, :]
    w_in_b, w_pa_b, w_pb_b, w_pc_b, w_o_b = (w.astype(BF16) for w in (w_in, w_pa, w_pb, w_pc, w_o))
    norm_g3, conv_b3, cln_g3, cln_b3 = row3(norm_g), row3(conv_b), row3(conv_ln_g), row3(conv_ln_b)
    sln_g3, sln_b3, sub_g3 = row3(sgu_ln_g), row3(sgu_ln_b), row3(diff_norm_g)
    lq1, lk1, lq2, lk2 = row3(lam_q1), row3(lam_k1), row3(lam_q2), row3(lam_k2)
    sgu_bt = jnp.swapaxes(sgu_b, 1, 2)
    fn_g = final_norm_g[None, :]
    col_scale = jnp.ones((1, IN_COLS), F32).at[:, COL_Q * LANE:COL_K * LANE].set(Q_SCALE)

    for layer in range(depth):
        lambda_init = 0.8 - 0.6 * math.exp(-0.3 * layer)
        proj = _inproj(x.reshape(b * s, d), norm_g3, w_in_b, col_scale, layer).reshape(b, s, IN_COLS)
        o = _attention(proj, lq1, lk1, lq2, lk2, sub_g3, layer, lambda_init)
        x = _merge(x, proj, o, conv_w, conv_b3, cln_g3, cln_b3, sln_g3, sln_b3, sgu_w, sgu_bt,
                   w_pa_b, w_pb_b, w_pc_b, w_o_b, fn_g, layer, final=(layer == depth - 1))
    return x
```

```python
import functools
import math

import jax
import jax.numpy as jnp
from jax import lax
from jax.experimental import pallas as pl
from jax.experimental.pallas import tpu as pltpu

D_MODEL = 1024
BRANCH_W = D_MODEL // 2
CONV_K = 31
DA_HEADS = 4
DA_HEAD_DIM = 64
DA_V_DIM = 128
SGU_GROUPS = 4
CHUNK = 128
IN_COLS = 8192
EPS = 1e-6

LANE = 128
SUBLANE = 8
COL_AGLU = 0
COL_AGATE = 1024 // LANE
COL_Q = 1536 // LANE
COL_K = 2048 // LANE
COL_V = 2560 // LANE
COL_BGATE = 3072 // LANE
COL_U = 3584 // LANE
COL_SV = 4096 // LANE
COL_CGATE = 4608 // LANE
COL_GATES = 5120 // LANE

ONES_ROWS = 16
Q_SCALE = DA_HEAD_DIM ** -0.5 * math.log2(math.e)
HALO = 32
NEG = -1e30

F32 = jnp.float32
BF16 = jnp.bfloat16

VMEM_LIMIT = 56 * 1024 * 1024


def _inproj_kernel(x_ref, g_ref, w_ref, cs_ref, o_ref, h_scr):
    @pl.when(pl.program_id(1) == 0)
    def _():
        x = x_ref[...]
        ms = jnp.mean(x * x, axis=-1, keepdims=True)
        h_scr[...] = (x * lax.rsqrt(ms + EPS) * g_ref[...]).astype(BF16)

    acc = jnp.dot(h_scr[...], w_ref[...], preferred_element_type=F32)
    o_ref[...] = (acc * cs_ref[...]).astype(BF16)


def _inproj(x2d, g, w_bf16, col_scale, layer, tm=2048, tn=2048):
    m = x2d.shape[0]
    return pl.pallas_call(
        _inproj_kernel,
        grid=(m // tm, IN_COLS // tn),
        in_specs=[
            pl.BlockSpec((tm, D_MODEL), lambda i, j: (i, 0)),
            pl.BlockSpec((None, 1, D_MODEL), lambda i, j: (layer, 0, 0)),
            pl.BlockSpec((None, D_MODEL, tn), lambda i, j: (layer, 0, j)),
            pl.BlockSpec((1, tn), lambda i, j: (0, j)),
        ],
        out_specs=pl.BlockSpec((tm, tn), lambda i, j: (i, j)),
        out_shape=jax.ShapeDtypeStruct((m, IN_COLS), BF16),
        scratch_shapes=[pltpu.VMEM((tm, D_MODEL), BF16)],
        compiler_params=pltpu.CompilerParams(
            dimension_semantics=("arbitrary", "arbitrary"), vmem_limit_bytes=VMEM_LIMIT),
        name="inproj",
    )(x2d, g, w_bf16, col_scale)


def _attn_kernel(lq1_ref, lk1_ref, lq2_ref, lk2_ref, sg_ref, q_ref, k_ref, v_ref, o_ref,
                 vt_scr, acc_scr, m_scr, sa_scr, sb_scr, *, lambda_init, tq, tk):
    i = pl.program_id(1)
    n_kv = v_ref.shape[0] // tk
    n_chain = 2 * DA_HEADS
    n_diag = tq // tk

    @pl.when(i == 0)
    def _():
        ones = jnp.ones((ONES_ROWS, tk), BF16)

        def fill(c, carry):
            vblk = v_ref[pl.ds(pl.multiple_of(c * tk, tk), tk), :]
            vt = vblk.astype(F32).T.astype(BF16)
            for h in range(DA_HEADS):
                vt_scr[c, h, 0:DA_V_DIM, :] = vt[h * DA_V_DIM:(h + 1) * DA_V_DIM, :]
                vt_scr[c, h, DA_V_DIM:, :] = ones
            return carry
        lax.fori_loop(0, n_kv, fill, 0)

    lam = (jnp.exp(jnp.sum(lq1_ref[...] * lk1_ref[...], axis=-1, keepdims=True))
           - jnp.exp(jnp.sum(lq2_ref[...] * lk2_ref[...], axis=-1, keepdims=True))
           + lambda_init)

    qt = q_ref[...].astype(F32).T
    sub = lax.broadcasted_iota(jnp.int32, (LANE, tq), 0)
    qtz = []
    for h in range(DA_HEADS):
        qth = qt[h * LANE:(h + 1) * LANE, :]
        qtz.append(jnp.where(sub < DA_HEAD_DIM, qth, 0.0).astype(BF16))
        qtz.append(jnp.where(sub >= DA_HEAD_DIM, qth, 0.0).astype(BF16))

    rel = (lax.broadcasted_iota(jnp.int32, (tk, tq), 0)
           - lax.broadcasted_iota(jnp.int32, (tk, tq), 1))

    acc_scr[...] = jnp.zeros(acc_scr.shape, F32)
    m_scr[...] = jnp.full(m_scr.shape, NEG, F32)

    def scores_into(j, dst, q0=0):
        kb = k_ref[pl.ds(pl.multiple_of(j * tk, tk), tk), :]
        for c in range(n_chain):
            dst[c, :, q0:] = jnp.dot(kb[:, (c // 2) * LANE:(c // 2 + 1) * LANE], qtz[c][:, q0:],
                                     preferred_element_type=F32)

    def softmax_pv(j, src, diag):
        q0 = 0 if diag is None else diag * tk
        pts, alphas = [], []
        for c in range(n_chain):
            st = src[c, :, q0:]
            if diag is not None:
                st = jnp.where(rel[:, q0:] <= -diag * tk, st, NEG)
            m = m_scr[c, :, q0:]
            m_new = jnp.maximum(m, jnp.max(st, axis=0, keepdims=True))
            m_scr[c, :, q0:] = m_new
            alphas.append(jnp.exp2(m - m_new))
            pts.append(jnp.exp2(st - m_new).astype(BF16))
        for c in range(n_chain):
            acc_scr[c, :, q0:] = alphas[c] * acc_scr[c, :, q0:] + jnp.dot(
                vt_scr[j, c // 2], pts[c], preferred_element_type=F32)

    bufs = (sa_scr, sb_scr)
    scores_into(0, sa_scr)

    def pair(j):
        scores_into(j + 1, sb_scr)
        softmax_pv(j, sa_scr, None)
        scores_into(j + 2, sa_scr)
        softmax_pv(j + 1, sb_scr, None)

    def quad(t, carry):
        pair(4 * t)
        pair(4 * t + 2)
        return carry
    n_quad = lax.div(i, 2)
    lax.fori_loop(0, n_quad, quad, 0)

    first = n_diag * i

    def diagonal():
        for d in range(n_diag):
            if d + 1 < n_diag:
                scores_into(first + d + 1, bufs[(d + 1) % 2], (d + 1) * tk)
            softmax_pv(first + d, bufs[d % 2], d)

    @pl.when(lax.rem(i, 2) == 1)
    def _():
        pair(4 * n_quad)
        diagonal()

    @pl.when(lax.rem(i, 2) == 0)
    def _():
        diagonal()

    def normalised(c):
        return acc_scr[c, 0:DA_V_DIM, :] * (1.0 / acc_scr[c, DA_V_DIM:DA_V_DIM + 1, :])

    for h in range(DA_HEADS):
        ot = normalised(2 * h) - lam * normalised(2 * h + 1)
        o = ot.T
        ms = jnp.mean(o * o, axis=-1, keepdims=True)
        o = o * lax.rsqrt(ms + EPS) * sg_ref[...] * (1.0 - lambda_init)
        o_ref[:, h * DA_V_DIM:(h + 1) * DA_V_DIM] = o.astype(o_ref.dtype)


def _attention(proj3d, lq1, lk1, lq2, lk2, sub_g, layer, lambda_init, tq=512, tk=256):
    b, s, _ = proj3d.shape
    assert tq == 2 * tk and s % tq == 0
    vec = lambda n: pl.BlockSpec((None, 1, n), lambda bi, i: (layer, 0, 0))
    lanes_per_branch = BRANCH_W // LANE
    n_chain = 2 * DA_HEADS
    return pl.pallas_call(
        functools.partial(_attn_kernel, lambda_init=lambda_init, tq=tq, tk=tk),
        grid=(b, s // tq),
        in_specs=[
            vec(DA_HEAD_DIM), vec(DA_HEAD_DIM), vec(DA_HEAD_DIM), vec(DA_HEAD_DIM), vec(DA_V_DIM),
            pl.BlockSpec((None, tq, BRANCH_W), lambda bi, i: (bi, i, COL_Q // lanes_per_branch)),
            pl.BlockSpec((None, s, BRANCH_W), lambda bi, i: (bi, 0, COL_K // lanes_per_branch)),
            pl.BlockSpec((None, s, BRANCH_W), lambda bi, i: (bi, 0, COL_V // lanes_per_branch)),
        ],
        out_specs=pl.BlockSpec((None, tq, BRANCH_W), lambda bi, i: (bi, i, 0)),
        out_shape=jax.ShapeDtypeStruct((b, s, BRANCH_W), BF16),
        scratch_shapes=[pltpu.VMEM((s // tk, DA_HEADS, DA_V_DIM + ONES_ROWS, tk), BF16),
                        pltpu.VMEM((n_chain, DA_V_DIM + ONES_ROWS, tq), F32),
                        pltpu.VMEM((n_chain, 1, tq), F32),
                        pltpu.VMEM((n_chain, tk, tq), F32),
                        pltpu.VMEM((n_chain, tk, tq), F32)],
        compiler_params=pltpu.CompilerParams(
            dimension_semantics=("arbitrary", "arbitrary"),
            vmem_limit_bytes=VMEM_LIMIT),
        name="diff_attn",
    )(lq1, lk1, lq2, lk2, sub_g, proj3d, proj3d, proj3d)


def _layer_norm(v, g, b):
    mu = jnp.mean(v, axis=-1, keepdims=True)
    d = v - mu
    var = jnp.mean(d * d, axis=-1, keepdims=True)
    return d * lax.rsqrt(var + EPS) * g + b


def _silu(v):
    return v * jax.nn.sigmoid(v)


def _merge_kernel(x_ref, aglu_ref, agate_ref, bgate_ref, u_ref, sv_ref, cgate_ref,
                  ga_ref, gb_ref, gc_ref, o_ref,
                  convw_ref, convb_ref, clng_ref, clnb_ref, slng_ref, slnb_ref, sguw_ref, sgub_ref,
                  wpa_ref, wpb_ref, wpc_ref, wo_ref, fng_ref, out_ref, zbuf, zsh, conv_scr,
                  *, t_rows, final):
    @pl.when(pl.program_id(1) == 0)
    def _():
        zbuf[0:HALO, :] = jnp.zeros((HALO, BRANCH_W), F32)

    zb = o_ref[...].astype(F32) * _silu(bgate_ref[...].astype(F32))
    yb = jnp.dot(zb.astype(BF16), wpb_ref[...], preferred_element_type=F32)

    v = _layer_norm(sv_ref[...].astype(F32), slng_ref[...], slnb_ref[...]).astype(BF16)
    row = lax.broadcasted_iota(jnp.int32, (CHUNK, CHUNK), 0)
    col = lax.broadcasted_iota(jnp.int32, (CHUNK, CHUNK), 1)
    tril = col <= row
    ws = [jnp.where(tril, sguw_ref[gi], 0.0).astype(BF16) for gi in range(SGU_GROUPS)]
    bs = [jnp.broadcast_to(sgub_ref[:, gi:gi + 1], (CHUNK, LANE)) for gi in range(SGU_GROUPS)]
    rows = []
    for n in range(t_rows // CHUNK):
        blocks = []
        for gi in range(SGU_GROUPS):
            vb = v[n * CHUNK:(n + 1) * CHUNK, gi * LANE:(gi + 1) * LANE]
            blocks.append(jnp.dot(ws[gi], vb, preferred_element_type=F32) + bs[gi])
        rows.append(jnp.concatenate(blocks, axis=1))
    mixed = jnp.concatenate(rows, axis=0)
    zc = u_ref[...].astype(F32) * mixed * _silu(cgate_ref[...].astype(F32))
    yc = jnp.dot(zc.astype(BF16), wpc_ref[...], preferred_element_type=F32)

    a = aglu_ref[:, :BRANCH_W].astype(F32)
    g = aglu_ref[:, BRANCH_W:].astype(F32)
    zbuf[HALO:HALO + t_rows, :] = a * jax.nn.sigmoid(g)
    rc = 64
    n_sh = t_rows + HALO - SUBLANE
    for c in range(BRANCH_W // LANE):
        cs = slice(c * LANE, (c + 1) * LANE)
        for sh in range(1, SUBLANE):
            zsh[sh - 1] = zbuf[sh:sh + n_sh, cs]
        for r in range(t_rows // rc):
            acc = jnp.broadcast_to(convb_ref[:, cs], (rc, LANE))
            for j in range(CONV_K):
                off = HALO - (CONV_K - 1) + j
                sh = off % SUBLANE
                base = r * rc + off - sh
                if sh == 0:
                    src = zbuf[base:base + rc, cs]
                else:
                    src = zsh[sh - 1, base:base + rc, :]
                acc = acc + convw_ref[j:j + 1, cs] * src
            conv_scr[r * rc:(r + 1) * rc, cs] = acc
    zbuf[0:HALO, :] = zbuf[t_rows:t_rows + HALO, :]
    za = _silu(_layer_norm(conv_scr[...], clng_ref[...], clnb_ref[...]))
    za = za * _silu(agate_ref[...].astype(F32))
    ya = jnp.dot(za.astype(BF16), wpa_ref[...], preferred_element_type=F32)

    ga = jax.nn.sigmoid(ga_ref[...].astype(F32))
    gb = jax.nn.sigmoid(gb_ref[...].astype(F32))
    gc = jax.nn.sigmoid(gc_ref[...].astype(F32))
    merged = ga * ya + gb * yb + gc * yc
    xn = x_ref[...] + jnp.dot(merged.astype(BF16), wo_ref[...], preferred_element_type=F32)
    if final:
        ms = jnp.mean(xn * xn, axis=-1, keepdims=True)
        xn = xn * lax.rsqrt(ms + EPS) * fng_ref[...]
    out_ref[...] = xn


def _merge(x3d, proj3d, o3d, conv_w, conv_b, cln_g, cln_b, sln_g, sln_b, sgu_w, sgu_bt,
           w_pa, w_pb, w_pc, w_o, fn_g, layer, final, t_rows=512):
    b, s, _ = x3d.shape

    def pcols(col, width):
        return pl.BlockSpec((None, t_rows, width), lambda bi, t: (bi, t, (col * LANE) // width))

    def lvec(n):
        return pl.BlockSpec((None, 1, n), lambda bi, t: (layer, 0, 0))

    def lmat(r, c):
        return pl.BlockSpec((None, r, c), lambda bi, t: (layer, 0, 0))

    return pl.pallas_call(
        functools.partial(_merge_kernel, t_rows=t_rows, final=final),
        grid=(b, s // t_rows),
        in_specs=[
            pl.BlockSpec((None, t_rows, D_MODEL), lambda bi, t: (bi, t, 0)),
            pcols(COL_AGLU, 2 * BRANCH_W),
            pcols(COL_AGATE, BRANCH_W),
            pcols(COL_BGATE, BRANCH_W),
            pcols(COL_U, BRANCH_W),
            pcols(COL_SV, BRANCH_W),
            pcols(COL_CGATE, BRANCH_W),
            pcols(COL_GATES, D_MODEL),
            pcols(COL_GATES + D_MODEL // LANE, D_MODEL),
            pcols(COL_GATES + 2 * D_MODEL // LANE, D_MODEL),
            pl.BlockSpec((None, t_rows, BRANCH_W), lambda bi, t: (bi, t, 0)),
            lmat(CONV_K, BRANCH_W),
            lvec(BRANCH_W), lvec(BRANCH_W), lvec(BRANCH_W), lvec(BRANCH_W), lvec(BRANCH_W),
            pl.BlockSpec((None, SGU_GROUPS, CHUNK, CHUNK), lambda bi, t: (layer, 0, 0, 0)),
            lmat(CHUNK, SGU_GROUPS),
            lmat(BRANCH_W, D_MODEL), lmat(BRANCH_W, D_MODEL), lmat(BRANCH_W, D_MODEL),
            lmat(D_MODEL, D_MODEL),
            pl.BlockSpec((1, D_MODEL), lambda bi, t: (0, 0)),
        ],
        out_specs=pl.BlockSpec((None, t_rows, D_MODEL), lambda bi, t: (bi, t, 0)),
        out_shape=jax.ShapeDtypeStruct((b, s, D_MODEL), F32),
        scratch_shapes=[pltpu.VMEM((t_rows + HALO, BRANCH_W), F32),
                        pltpu.VMEM((SUBLANE - 1, t_rows + HALO - SUBLANE, LANE), F32),
                        pltpu.VMEM((t_rows, BRANCH_W), F32)],
        compiler_params=pltpu.CompilerParams(
            dimension_semantics=("arbitrary", "arbitrary"), vmem_limit_bytes=VMEM_LIMIT),
        name="merge",
    )(x3d, proj3d, proj3d, proj3d, proj3d, proj3d, proj3d, proj3d, proj3d, proj3d, o3d,
      conv_w, conv_b, cln_g, cln_b, sln_g, sln_b, sgu_w, sgu_bt, w_pa, w_pb, w_pc, w_o, fn_g)


def kernel(x, norm_g, w_in, conv_w, conv_b, conv_ln_g, conv_ln_b, lam_q1, lam_k1, lam_q2, lam_k2,
           diff_norm_g, sgu_ln_g, sgu_ln_b, sgu_w, sgu_b, w_pa, w_pb, w_pc, w_o, final_norm_g):
    b, s, d = x.shape
    depth = w_in.shape[0]
    row3 = lambda p: p[:, None, :]
    w_in_b, w_pa_b, w_pb_b, w_pc_b, w_o_b = (w.astype(BF16) for w in (w_in, w_pa, w_pb, w_pc, w_o))
    norm_g3, conv_b3, cln_g3, cln_b3 = row3(norm_g), row3(conv_b), row3(conv_ln_g), row3(conv_ln_b)
    sln_g3, sln_b3, sub_g3 = row3(sgu_ln_g), row3(sgu_ln_b), row3(diff_norm_g)
    lq1, lk1, lq2, lk2 = row3(lam_q1), row3(lam_k1), row3(lam_q2), row3(lam_k2)
    sgu_bt = jnp.swapaxes(sgu_b, 1, 2)
    fn_g = final_norm_g[None, :]
    col_scale = jnp.ones((1, IN_COLS), F32).at[:, COL_Q * LANE:COL_K * LANE].set(Q_SCALE)

    for layer in range(depth):
        lambda_init = 0.8 - 0.6 * math.exp(-0.3 * layer)
        proj = _inproj(x.reshape(b * s, d), norm_g3, w_in_b, col_scale, layer).reshape(b, s, IN_COLS)
        o = _attention(proj, lq1, lk1, lq2, lk2, sub_g3, layer, lambda_init)
        x = _merge(x, proj, o, conv_w, conv_b3, cln_g3, cln_b3, sln_g3, sln_b3, sgu_w, sgu_bt,
                   w_pa_b, w_pb_b, w_pc_b, w_o_b, fn_g, layer, final=(layer == depth - 1))
    return x
```
